```python
import math
import jax, jax.numpy as jnp
from jax import lax
import numpy as np

D_MODEL = 1024
BATCH = 4
SEQ = 4096
DEPTH = 2
DEC_BATCH = 128
DEC_SEQ = 4
PAST_LEN = 8192
PAGE_SIZE = 128

N_EVEN = (DEPTH + 1) // 2
N_ODD = DEPTH // 2
H_A = 4
HD_A = 64
QK_A = H_A * 2 * HD_A
V_A = H_A * 2 * HD_A
D_RNN = 512
NB_RNN = 8
BW_RNN = D_RNN // NB_RNN
CONV_RNN = 4
LRU_C = 8.0
IN_E = 2 * QK_A + V_A + 2 * D_RNN
MIX_E = V_A + D_RNN
H_C = 16
Q_RANK = 256
KV_RANK = 128
NOPE_C = 64
ROPE_C = 32
V_C = 64
LAT_C = KV_RANK + ROPE_C
IN_C = Q_RANK + KV_RANK + ROPE_C
ROPE_THETA = 10000.0
MLA_SCALE = 1.0 / math.sqrt(NOPE_C + ROPE_C)
DIFF_SCALE = 1.0 / math.sqrt(HD_A)
D_FF = 2816
CONV_FF = 3
Q_BLOCK = 128
EPS = 1e-6

kernel_name = 'diffattn_rglru_mla_convffn_step'

F32 = jnp.float32


def rms_norm(x, g):
    xf = x.astype(F32)
    y = xf * lax.rsqrt(jnp.mean(xf * xf, axis=-1, keepdims=True) + EPS)
    return (y * g.astype(F32)).astype(x.dtype)


def causal_dwconv(u, hist, w, b):
    K = w.shape[0]
    T = u.shape[1]
    full = jnp.concatenate([hist.astype(u.dtype), u], axis=1)
    out = b + full[:, 0:T] * w[0]
    for j in range(1, K):
        out = out + full[:, j:j + T] * w[j]
    return out, full[:, T:]


def rope(x, pos):
    half = ROPE_C // 2
    inv = ROPE_THETA ** (-jnp.arange(half, dtype=F32) / half)
    ang = pos.astype(F32)[:, None] * inv[None, :]
    ang = ang.reshape((ang.shape[0],) + (1,) * (x.ndim - 3) + (half,))
    c, s = jnp.cos(ang), jnp.sin(ang)
    xf = x.astype(F32)
    x1, x2 = xf[..., :half], xf[..., half:]
    return jnp.concatenate([x1 * c - x2 * s, x1 * s + x2 * c], axis=-1).astype(x.dtype)


def sweep_query_blocks(q, attend):
    B, T = q.shape[:2]
    nb = T // Q_BLOCK
    qb = jnp.swapaxes(q.reshape((B, nb, Q_BLOCK) + q.shape[2:]), 0, 1)
    pos = jnp.arange(T, dtype=jnp.int32).reshape(nb, Q_BLOCK)
    o = lax.map(lambda a: attend(a[0], a[1]), (qb, pos))
    o = jnp.swapaxes(o, 0, 1)
    return o.reshape((B, T) + o.shape[3:])


def diff_lambda(lv, lam_init):
    lv = lv.astype(F32)
    return jnp.exp(jnp.sum(lv[0] * lv[1])) - jnp.exp(jnp.sum(lv[2] * lv[3])) + lam_init


def diff_qkv(z, qk_g):
    lead = z.shape[:-1]
    q = rms_norm(z[..., :QK_A].reshape(lead + (H_A, 2, HD_A)), qk_g[0])
    k = rms_norm(z[..., QK_A:2 * QK_A].reshape(lead + (H_A, 2, HD_A)), qk_g[1])
    v = z[..., 2 * QK_A:2 * QK_A + V_A].reshape(lead + (H_A, 2 * HD_A))
    return q, k, v


def diff_attn_core(q, k, v, mask, lam):
    s = jnp.einsum('bqhcd,bkhcd->bhcqk', q, k).astype(F32) * DIFF_SCALE
    p = jax.nn.softmax(jnp.where(mask, s, -jnp.inf), axis=-1)
    w = (p[:, :, 0] - lam * p[:, :, 1]).astype(v.dtype)
    return jnp.einsum('bhqk,bkhe->bqhe', w, v)


def softmax_attn_core(q, k, v, mask, scale):
    s = jnp.einsum('bqhd,bkhd->bhqk', q, k).astype(F32) * scale
    p = jax.nn.softmax(jnp.where(mask, s, -jnp.inf), axis=-1).astype(v.dtype)
    return jnp.einsum('bhqk,bkhd->bqhd', p, v)


def _linear_combine(c1, c2):
    a1, b1 = c1
    a2, b2 = c2
    return a1 * a2, a2 * b1 + b2


def rg_lru(xc, h0, gate_w, gate_b, lam_lru, reset_first):
    B, T, _ = xc.shape
    g = jnp.einsum('btni,gnij->gbtnj', xc.reshape(B, T, NB_RNN, BW_RNN), gate_w).reshape(2, B, T, D_RNN)
    g = g.astype(F32) + gate_b.astype(F32)[:, None, None, :]
    r = jax.nn.sigmoid(g[0])
    i = jax.nn.sigmoid(g[1])
    log_a = -LRU_C * r * jax.nn.softplus(-lam_lru.astype(F32))
    a = jnp.exp(log_a)
    mult = jnp.sqrt(-jnp.expm1(2.0 * log_a))
    if reset_first:
        mult = mult.at[:, 0].set(1.0)
    b = mult * i * xc.astype(F32)
    b = b.at[:, 0].add(a[:, 0] * h0.astype(F32))
    _, h = lax.associative_scan(_linear_combine, (a, b), axis=1)
    return h.astype(xc.dtype), h[:, -1].astype(xc.dtype)


def even_merge(z, attn, subln_g, lam_init, conv_hist, h0, reset_first, conv_w, conv_b, gate_w, gate_b, lam_lru, w_out):
    lead = z.shape[:-1]
    a_out = (rms_norm(attn, subln_g) * (1.0 - lam_init)).reshape(lead + (V_A,))
    off = 2 * QK_A + V_A
    xr = z[..., off:off + D_RNN]
    gr = z[..., off + D_RNN:]
    xc, conv_new = causal_dwconv(xr, conv_hist, conv_w, conv_b)
    hseq, h_last = rg_lru(xc, h0, gate_w, gate_b, lam_lru, reset_first)
    y_rnn = hseq * jax.nn.gelu(gr, approximate=True)
    return jnp.concatenate([a_out, y_rnn], axis=-1) @ w_out, conv_new, h_last


def mla_query_rows(h, w_in, q_a_g, kv_a_g, w_uq, qn_g, rn_g, pos):
    z = h @ w_in
    cq = rms_norm(z[..., :Q_RANK], q_a_g)
    q = jnp.einsum('...r,rhe->...he', cq, w_uq)
    q_nope = rms_norm(q[..., :NOPE_C], qn_g[0])
    q_rope = rope(rms_norm(q[..., NOPE_C:], rn_g[0]), pos)
    ckv = rms_norm(z[..., Q_RANK:Q_RANK + KV_RANK], kv_a_g)
    k_rope = rope(rms_norm(z[..., Q_RANK + KV_RANK:], rn_g[1]), pos)
    return jnp.concatenate([q_nope, q_rope], axis=-1), jnp.concatenate([ckv, k_rope], axis=-1)


def mla_keys(rows, w_ukv, qn_g):
    k_nope = rms_norm(jnp.einsum('...r,rhe->...he', rows[..., :KV_RANK], w_ukv[..., :NOPE_C]), qn_g[1])
    kr = jnp.broadcast_to(rows[..., None, KV_RANK:], k_nope.shape[:-1] + (ROPE_C,))
    return jnp.concatenate([k_nope, kr], axis=-1)


def conv_ffn(h, w_up, cw, cb, w_down, hist):
    u, new_hist = causal_dwconv(h @ w_up, hist, cw, cb)
    return (jax.nn.silu(u[..., :D_FF]) * u[..., D_FF:]) @ w_down, new_hist


def setup_inputs(seed: int = 0) -> dict:
    key = jax.random.key(seed)
    it = iter(jax.random.split(key, 48))
    n_pages = PAST_LEN // PAGE_SIZE
    n_pool = (DEC_BATCH * n_pages * 5) // 4

    def nrm(shape, scale):
        return jax.random.normal(next(it), shape, F32) * scale

    def gain(shape):
        return 1.0 + nrm(shape, 0.02)

    a0 = jax.random.uniform(next(it), (N_EVEN, D_RNN), F32, 0.9, 0.999)
    page_table = jax.random.permutation(next(it), n_pool)[:DEC_BATCH * n_pages].reshape(DEC_BATCH, n_pages).astype(jnp.int32)
    return {
        'x_prompt': nrm((BATCH, SEQ, D_MODEL), 1.0),
        'x_sample': nrm((DEC_BATCH, DEC_SEQ, D_MODEL), 1.0),
        'cache_kv_diff': jax.random.normal(next(it), (N_EVEN, n_pool, PAGE_SIZE, 2, H_A, 2 * HD_A), F32),
        'cache_mla': jax.random.normal(next(it), (N_ODD, n_pool, PAGE_SIZE, LAT_C), F32),
        'state_lru_h': nrm((N_EVEN, DEC_BATCH, D_RNN), 0.5),
        'state_lru_conv': nrm((N_EVEN, DEC_BATCH, CONV_RNN - 1, D_RNN), 1.0),
        'state_ffn_conv': nrm((DEPTH, DEC_BATCH, CONV_FF - 1, 2 * D_FF), 1.0),
        'page_table': page_table,
        'norm_mix': gain((DEPTH, D_MODEL)),
        'norm_ffn': gain((DEPTH, D_MODEL)),
        'w_in_e': nrm((N_EVEN, D_MODEL, IN_E), D_MODEL ** -0.5),
        'qk_norm_a': gain((N_EVEN, 2, 2, HD_A)),
        'lambda_a': nrm((N_EVEN, 4, HD_A), 0.1),
        'subln_a': gain((N_EVEN, 2 * HD_A)),
        'conv_w_lru': nrm((N_EVEN, CONV_RNN, D_RNN), CONV_RNN ** -0.5),
        'conv_b_lru': nrm((N_EVEN, D_RNN), 0.01),
        'gate_w_lru': nrm((N_EVEN, 2, NB_RNN, BW_RNN, BW_RNN), BW_RNN ** -0.5),
        'gate_b_lru': nrm((N_EVEN, 2, D_RNN), 0.01),
        'lambda_lru': jnp.log(a0) - jnp.log1p(-a0),
        'w_out_e': nrm((N_EVEN, MIX_E, D_MODEL), MIX_E ** -0.5),
        'w_in_c': nrm((N_ODD, D_MODEL, IN_C), D_MODEL ** -0.5),
        'q_a_norm': gain((N_ODD, Q_RANK)),
        'kv_a_norm': gain((N_ODD, KV_RANK)),
        'w_uq': nrm((N_ODD, Q_RANK, H_C, NOPE_C + ROPE_C), Q_RANK ** -0.5),
        'w_ukv': nrm((N_ODD, KV_RANK, H_C, NOPE_C + V_C), KV_RANK ** -0.5),
        'qn_c': gain((N_ODD, 2, NOPE_C)),
        'rn_c': gain((N_ODD, 2, ROPE_C)),
        'w_out_c': nrm((N_ODD, H_C * V_C, D_MODEL), (H_C * V_C) ** -0.5),
        'w_up': nrm((DEPTH, D_MODEL, 2 * D_FF), D_MODEL ** -0.5),
        'conv_w_ffn': nrm((DEPTH, CONV_FF, 2 * D_FF), CONV_FF ** -0.5),
        'conv_b_ffn': nrm((DEPTH, 2 * D_FF), 0.01),
        'w_down': nrm((DEPTH, D_FF, D_MODEL), D_FF ** -0.5),
    }


def reference(x_prompt, x_sample, cache_kv_diff, cache_mla, state_lru_h, state_lru_conv, state_ffn_conv, page_table,
              norm_mix, norm_ffn, w_in_e, qk_norm_a, lambda_a, subln_a, conv_w_lru, conv_b_lru, gate_w_lru, gate_b_lru,
              lambda_lru, w_out_e, w_in_c, q_a_norm, kv_a_norm, w_uq, w_ukv, qn_c, rn_c, w_out_c,
              w_up, conv_w_ffn, conv_b_ffn, w_down):
    B, T, _ = x_prompt.shape
    DB, S, _ = x_sample.shape
    n_past = page_table.shape[1] * PAGE_SIZE
    pos_p = jnp.arange(T, dtype=jnp.int32)
    pos_s = n_past + jnp.arange(S, dtype=jnp.int32)
    mask_s = jnp.arange(n_past + S, dtype=jnp.int32)[None, :] <= pos_s[:, None]
    xp, xs = x_prompt, x_sample
    kv_p, kv_s, mla_p, mla_s = [], [], [], []
    lh_p, lh_s, lc_p, lc_s, fc_p, fc_s = [], [], [], [], [], []
    for l in range(DEPTH):
        hp = rms_norm(xp, norm_mix[l])
        hs = rms_norm(xs, norm_mix[l])
        if l % 2 == 0:
            e = l // 2
            lam_init = 0.8 - 0.6 * math.exp(-0.3 * l)
            lam = diff_lambda(lambda_a[e], lam_init)
            zp = hp @ w_in_e[e]
            zs = hs @ w_in_e[e]
            qp, kp, vp = diff_qkv(zp, qk_norm_a[e])
            qs, ks_, vs = diff_qkv(zs, qk_norm_a[e])
            ap = sweep_query_blocks(
                qp, lambda qb, qpos: diff_attn_core(qb, kp, vp, pos_p[None, :] <= qpos[:, None], lam))

            def diff_sample_seq(args):
                qb, kb, vb, pt = args
                past = cache_kv_diff[e, pt].reshape(n_past, 2, H_A, 2 * HD_A)
                k = jnp.concatenate([past[:, 0].reshape(n_past, H_A, 2, HD_A), kb], axis=0)
                v = jnp.concatenate([past[:, 1], vb], axis=0)
                return diff_attn_core(qb[None], k[None], v[None], mask_s, lam)[0]

            a_s = lax.map(diff_sample_seq, (qs, ks_, vs, page_table))
            zero_conv = jnp.zeros((B, CONV_RNN - 1, D_RNN), xp.dtype)
            zero_h = jnp.zeros((B, D_RNN), xp.dtype)
            mp, cpn, hpn = even_merge(zp, ap, subln_a[e], lam_init, zero_conv, zero_h, True, conv_w_lru[e],
                                      conv_b_lru[e], gate_w_lru[e], gate_b_lru[e], lambda_lru[e], w_out_e[e])
            ms, csn, hsn = even_merge(zs, a_s, subln_a[e], lam_init, state_lru_conv[e], state_lru_h[e], False,
                                      conv_w_lru[e], conv_b_lru[e], gate_w_lru[e], gate_b_lru[e], lambda_lru[e],
                                      w_out_e[e])
            kv_p.append(jnp.stack([kp.reshape(B, T, H_A, 2 * HD_A), vp], axis=2))
            kv_s.append(jnp.stack([ks_.reshape(DB, S, H_A, 2 * HD_A), vs], axis=2))
            lc_p.append(cpn)
            lc_s.append(csn)
            lh_p.append(hpn)
            lh_s.append(hsn)
        else:
            o = l // 2
            qp, rows_p = mla_query_rows(hp, w_in_c[o], q_a_norm[o], kv_a_norm[o], w_uq[o], qn_c[o], rn_c[o], pos_p)
            qs, rows_s = mla_query_rows(hs, w_in_c[o], q_a_norm[o], kv_a_norm[o], w_uq[o], qn_c[o], rn_c[o], pos_s)
            kp = mla_keys(rows_p, w_ukv[o], qn_c[o])
            vp = jnp.einsum('btr,rhe->bthe', rows_p[..., :KV_RANK], w_ukv[o][..., NOPE_C:])
            ap = sweep_query_blocks(
                qp, lambda qb, qpos: softmax_attn_core(qb, kp, vp, pos_p[None, :] <= qpos[:, None], MLA_SCALE))

            def mla_sample_seq(args):
                qb, rb, pt = args
                rows = jnp.concatenate([cache_mla[o, pt].reshape(n_past, LAT_C), rb], axis=0)
                k = mla_keys(rows, w_ukv[o], qn_c[o])
                s = jnp.einsum('qhd,khd->hqk', qb, k).astype(F32) * MLA_SCALE
                p = jax.nn.softmax(jnp.where(mask_s, s, -jnp.inf), axis=-1).astype(rows.dtype)
                lat = jnp.einsum('hqk,kr->qhr', p, rows[:, :KV_RANK])
                return jnp.einsum('qhr,rhe->qhe', lat, w_ukv[o][..., NOPE_C:])

            a_s = lax.map(mla_sample_seq, (qs, rows_s, page_table))
            mp = ap.reshape(B, T, H_C * V_C) @ w_out_c[o]
            ms = a_s.reshape(DB, S, H_C * V_C) @ w_out_c[o]
            mla_p.append(rows_p)
            mla_s.append(rows_s)
        xp = xp + mp
        xs = xs + ms
        gp = rms_norm(xp, norm_ffn[l])
        gs = rms_norm(xs, norm_ffn[l])
        fp, fcp = conv_ffn(gp, w_up[l], conv_w_ffn[l], conv_b_ffn[l], w_down[l],
                           jnp.zeros((B, CONV_FF - 1, 2 * D_FF), xp.dtype))
        fs, fcs = conv_ffn(gs, w_up[l], conv_w_ffn[l], conv_b_ffn[l], w_down[l], state_ffn_conv[l])
        xp = xp + fp
        xs = xs + fs
        fc_p.append(fcp)
        fc_s.append(fcs)
    return (xp, xs, jnp.stack(kv_p), jnp.stack(kv_s), jnp.stack(mla_p), jnp.stack(mla_s),
            jnp.stack(lh_p), jnp.stack(lh_s), jnp.stack(lc_p), jnp.stack(lc_s), jnp.stack(fc_p), jnp.stack(fc_s))
```

```python
import functools
import math

import jax
import jax.numpy as jnp
from jax import lax
from jax.experimental import pallas as pl
from jax.experimental.pallas import tpu as pltpu

F32 = jnp.float32
BF16 = jnp.bfloat16

EPS = 1e-6
NEG_BIG = -1e30
ROPE_THETA = 10000.0
LRU_C = 8.0
PAGE = 128

H_A = 4
HD_A = 64
D_RNN = 512
H_C = 16
Q_RANK = 256
KV_RANK = 128
NOPE_C = 64
ROPE_C = 32
V_C = 64
LAT_C = KV_RANK + ROPE_C
HEAD_PAD = 128
D_FF = 2816

VMEM_LIMIT_BYTES = 56 * 1024 * 1024


def _params(*semantics):
    return pltpu.CompilerParams(dimension_semantics=semantics,
                                vmem_limit_bytes=VMEM_LIMIT_BYTES)


def _row_tile(n, want):
    t = min(n, want)
    assert n % t == 0
    return t


def _const(shape):
    return pl.BlockSpec(shape, lambda *_: (0,) * len(shape))


def _rms(x, g):
    return x * lax.rsqrt(jnp.mean(x * x, axis=-1, keepdims=True) + EPS) * g


def _nt_dot(a, b):
    return lax.dot_general(a, b, (((1,), (1,)), ((), ())), preferred_element_type=F32)


def _dot(a, b):
    return jnp.dot(a, b, preferred_element_type=F32)


def _gelu_tanh(x):
    return 0.5 * x * (1.0 + jnp.tanh(math.sqrt(2.0 / math.pi) * (x + 0.044715 * x * x * x)))


def _softplus(y):
    return jnp.maximum(y, 0.0) + jnp.log1p(jnp.exp(-jnp.abs(y)))


def _shift_rows(x, prev8, s):
    xr = pltpu.roll(x, s, 0)
    row8 = lax.broadcasted_iota(jnp.int32, prev8.shape, 0)
    top = jnp.where(row8 < s, pltpu.roll(prev8, s, 0), xr[0:8])
    return jnp.concatenate([top, xr[8:]], axis=0)


def _even_proj_kernel(x_ref, g_ref, w_ref, qs_ref, ks_ref, gm_ref,
                      qb_ref, kb_ref, vb_ref, kv_ref, xg_ref):
    hb = _rms(x_ref[...], g_ref[...]).astype(BF16)
    z = _dot(hb, w_ref[...])
    zq, zk, zv = z[:, 0:512], z[:, 512:1024], z[:, 1024:1536]
    gm = gm_ref[...]
    mq = _dot((zq * zq).astype(BF16), gm)
    mk = _dot((zk * zk).astype(BF16), gm)
    q = zq * lax.rsqrt(mq + EPS) * qs_ref[...]
    k = zk * lax.rsqrt(mk + EPS) * ks_ref[...]
    qb_ref[...] = q.astype(BF16)
    kb_ref[...] = k.astype(BF16)
    vb_ref[...] = zv.astype(BF16)
    kv_ref[:, 0:512] = k
    kv_ref[:, 512:1024] = zv
    xg_ref[...] = z[:, 1536:2560]


def _even_proj(x, g, w, qs, ks, gm, tm):
    n = x.shape[0]
    tm = _row_tile(n, tm)
    row = lambda c: pl.BlockSpec((tm, c), lambda i: (i, 0))
    return pl.pallas_call(
        _even_proj_kernel,
        grid=(n // tm,),
        in_specs=[row(1024), _const((1, 1024)), _const((1024, 2560)), _const((1, 512)),
                  _const((1, 512)), _const((512, 512))],
        out_specs=[row(512), row(512), row(512), row(1024), row(1024)],
        out_shape=[jax.ShapeDtypeStruct((n, 512), BF16), jax.ShapeDtypeStruct((n, 512), BF16),
                   jax.ShapeDtypeStruct((n, 512), BF16), jax.ShapeDtypeStruct((n, 1024), F32),
                   jax.ShapeDtypeStruct((n, 1024), F32)],
        compiler_params=_params("parallel"),
        name="even_proj",
    )(x, g, w, qs, ks, gm)


def _online_update(s, m, l, acc, v):
    m_new = jnp.maximum(m, jnp.max(s, axis=-1, keepdims=True))
    alpha = jnp.exp(m - m_new)
    p = jnp.exp(s - m_new)
    l_new = alpha * l + jnp.sum(p, axis=-1, keepdims=True)
    acc_new = alpha * acc + _dot(p.astype(BF16), v)
    return m_new, l_new, acc_new


def _flash_kernel(*refs, diff, tb, lam_init):
    if diff:
        q_ref, k_ref, v_ref, lam_ref, sg_ref, o_ref = refs
    else:
        q_ref, k_ref, v_ref, o_ref = refs
    qi = pl.program_id(2)
    q = q_ref[...]
    lane = lax.broadcasted_iota(jnp.int32, (tb, 128), 1)
    if diff:
        zero = jnp.zeros_like(q)
        qa = jnp.where(lane < HD_A, q, zero)
        qb = jnp.where(lane >= HD_A, q, zero)
    else:
        qa, qb = q[:, 0:128], q[:, 128:256]

    def kv_block(j):
        start = pl.multiple_of(j * tb, tb)
        kblk = k_ref[pl.ds(start, tb), :]
        vblk = v_ref[pl.ds(start, tb), :]
        if diff:
            return kblk, kblk, vblk
        return kblk[:, 0:128], kblk[:, 128:256], vblk

    def body(j, carry):
        m1, l1, a1, m2, l2, a2 = carry
        ka, kb, vblk = kv_block(j)
        m1, l1, a1 = _online_update(_nt_dot(qa, ka), m1, l1, a1, vblk)
        m2, l2, a2 = _online_update(_nt_dot(qb, kb), m2, l2, a2, vblk)
        return m1, l1, a1, m2, l2, a2

    col1 = jnp.full((tb, 1), NEG_BIG, F32)
    zcol = jnp.zeros((tb, 1), F32)
    zacc = jnp.zeros((tb, 128), F32)
    carry = lax.fori_loop(0, qi, body, (col1, zcol, zacc, col1, zcol, zacc))

    m1, l1, a1, m2, l2, a2 = carry
    ka, kb, vblk = kv_block(qi)
    r = lax.broadcasted_iota(jnp.int32, (tb, tb), 0)
    c = lax.broadcasted_iota(jnp.int32, (tb, tb), 1)
    keep = c <= r
    m1, l1, a1 = _online_update(jnp.where(keep, _nt_dot(qa, ka), NEG_BIG), m1, l1, a1, vblk)
    m2, l2, a2 = _online_update(jnp.where(keep, _nt_dot(qb, kb), NEG_BIG), m2, l2, a2, vblk)

    o1 = a1 * (1.0 / l1)
    o2 = a2 * (1.0 / l2)
    if diff:
        lv = lam_ref[...]
        lam = (jnp.exp(jnp.sum(lv[0:1] * lv[1:2], axis=-1, keepdims=True))
               - jnp.exp(jnp.sum(lv[2:3] * lv[3:4], axis=-1, keepdims=True)) + lam_init)
        o = o1 - lam * o2
        o = _rms(o, sg_ref[...]) * (1.0 - lam_init)
    else:
        o = jnp.where(lane < V_C, o1, o2)
    o_ref[...] = o.astype(o_ref.dtype)


def _flash(q, k, v, extra, *, diff, batch, seq, tb, lam_init=0.0):
    n = q.shape[0]
    tb = _row_tile(seq, tb)
    nq = seq // tb
    qw = 128 if diff else 256
    groups = q.shape[1] // qw
    in_specs = [pl.BlockSpec((tb, qw), lambda b, g, i: (b * nq + i, g)),
                pl.BlockSpec((seq, qw), lambda b, g, i: (b, g)),
                pl.BlockSpec((seq, 128), lambda b, g, i: (b, g))]
    in_specs += [_const(e.shape) for e in extra]
    return pl.pallas_call(
        functools.partial(_flash_kernel, diff=diff, tb=tb, lam_init=lam_init),
        grid=(batch, groups, nq),
        in_specs=in_specs,
        out_specs=pl.BlockSpec((tb, 128), lambda b, g, i: (b * nq + i, g)),
        out_shape=jax.ShapeDtypeStruct((n, groups * 128), BF16),
        compiler_params=_params("parallel", "parallel", "arbitrary"),
        name="flash_diff" if diff else "flash_mla",
    )(q, k, v, *extra)


def _lru_gates(xc, gw_ref, gb_ref, lam_ref):
    g = _dot(xc.astype(BF16), gw_ref[...]) + gb_ref[...]
    r = jax.nn.sigmoid(g[:, 0:D_RNN])
    i = jax.nn.sigmoid(g[:, D_RNN:2 * D_RNN])
    a = jnp.exp(-LRU_C * r * _softplus(-lam_ref[...]))
    mult = jnp.sqrt(1.0 - a * a)
    return a, mult, i


def _lru_prompt_kernel(x_ref, gr_ref, cw_ref, cb_ref, gw_ref, gb_ref, lam_ref,
                       y_ref, hl_ref, ct_ref, xprev_s, h_s, *, tm):
    ti = pl.program_id(1)

    @pl.when(ti == 0)
    def _():
        xprev_s[...] = jnp.zeros_like(xprev_s)
        h_s[...] = jnp.zeros_like(h_s)

    x = x_ref[...]
    prev8 = xprev_s[...]
    cw = cw_ref[...]
    xc = (cb_ref[...] + cw[0:1] * _shift_rows(x, prev8, 3) + cw[1:2] * _shift_rows(x, prev8, 2)
          + cw[2:3] * _shift_rows(x, prev8, 1) + cw[3:4] * x)
    a, mult, i = _lru_gates(xc, gw_ref, gb_ref, lam_ref)
    row = lax.broadcasted_iota(jnp.int32, (tm, D_RNN), 0)
    mult = jnp.where(jnp.logical_and(row == 0, ti == 0), 1.0, mult)
    b = mult * i * xc
    d = 1
    while d < tm:
        ar = jnp.where(row >= d, pltpu.roll(a, d, 0), 1.0)
        br = jnp.where(row >= d, pltpu.roll(b, d, 0), 0.0)
        b = a * br + b
        a = a * ar
        d *= 2
    h = a * h_s[7:8] + b
    y_ref[...] = (h * _gelu_tanh(gr_ref[...])).astype(BF16)
    h_s[...] = h[tm - 8:tm]
    hl_ref[0] = h[tm - 8:tm]
    ct_ref[0] = x[tm - 8:tm]
    xprev_s[...] = x[tm - 8:tm]


def _lru_prompt(xg, cw, cb, gw, gb, lam, *, batch, seq, tm):
    n = xg.shape[0]
    tm = _row_tile(seq, tm)
    nt = seq // tm
    return pl.pallas_call(
        functools.partial(_lru_prompt_kernel, tm=tm),
        grid=(batch, nt),
        in_specs=[pl.BlockSpec((tm, D_RNN), lambda b, t: (b * nt + t, 0)),
                  pl.BlockSpec((tm, D_RNN), lambda b, t: (b * nt + t, 1)),
                  _const((4, D_RNN)), _const((1, D_RNN)), _const((D_RNN, 2 * D_RNN)),
                  _const((1, 2 * D_RNN)), _const((1, D_RNN))],
        out_specs=[pl.BlockSpec((tm, D_RNN), lambda b, t: (b * nt + t, 0)),
                   pl.BlockSpec((1, 8, D_RNN), lambda b, t: (b * nt + t, 0, 0)),
                   pl.BlockSpec((1, 8, D_RNN), lambda b, t: (b * nt + t, 0, 0))],
        out_shape=[jax.ShapeDtypeStruct((n, D_RNN), BF16),
                   jax.ShapeDtypeStruct((batch * nt, 8, D_RNN), F32),
                   jax.ShapeDtypeStruct((batch * nt, 8, D_RNN), F32)],
        scratch_shapes=[pltpu.VMEM((8, D_RNN), F32), pltpu.VMEM((8, D_RNN), F32)],
        compiler_params=_params("arbitrary", "arbitrary"),
        name="lru_prompt",
    )(xg, xg, cw, cb, gw, gb, lam)


def _lru_sample_kernel(x_ref, gr_ref, hist_ref, h0_ref, cw_ref, cb_ref, gw_ref, gb_ref, lam_ref,
                       y_ref, hl_ref, ct_ref, *, db, steps):
    n = steps * db
    ext = jnp.concatenate([hist_ref[...], x_ref[...]], axis=0)
    cw = cw_ref[...]
    xc = (cb_ref[...] + cw[0:1] * ext[0:n] + cw[1:2] * ext[db:db + n]
          + cw[2:3] * ext[2 * db:2 * db + n] + cw[3:4] * ext[3 * db:3 * db + n])
    a, mult, i = _lru_gates(xc, gw_ref, gb_ref, lam_ref)
    b = mult * i * xc
    h = h0_ref[...]
    hs = []
    for t in range(steps):
        h = a[t * db:(t + 1) * db] * h + b[t * db:(t + 1) * db]
        hs.append(h)
    hseq = jnp.concatenate(hs, axis=0)
    y_ref[...] = (hseq * _gelu_tanh(gr_ref[...])).astype(BF16)
    hl_ref[...] = h
    ct_ref[...] = ext[n:n + 3 * db]


def _lru_sample(xg, hist, h0, cw, cb, gw, gb, lam, *, db, steps):
    n = steps * db
    return pl.pallas_call(
        functools.partial(_lru_sample_kernel, db=db, steps=steps),
        grid=(1,),
        in_specs=[pl.BlockSpec((n, D_RNN), lambda i: (0, 0)),
                  pl.BlockSpec((n, D_RNN), lambda i: (0, 1)),
                  _const((3 * db, D_RNN)), _const((db, D_RNN)),
                  _const((4, D_RNN)), _const((1, D_RNN)), _const((D_RNN, 2 * D_RNN)),
                  _const((1, 2 * D_RNN)), _const((1, D_RNN))],
        out_specs=[_const((n, D_RNN)), _const((db, D_RNN)), _const((3 * db, D_RNN))],
        out_shape=[jax.ShapeDtypeStruct((n, D_RNN), BF16),
                   jax.ShapeDtypeStruct((db, D_RNN), F32),
                   jax.ShapeDtypeStruct((3 * db, D_RNN), F32)],
        compiler_params=_params("arbitrary"),
        name="lru_sample",
    )(xg, xg, hist, h0, cw, cb, gw, gb, lam)


def _mm_res_kernel(*refs, n_in):
    res_ref, o_ref = refs[0], refs[-1]
    acc = res_ref[...]
    for a_ref, w_ref in zip(refs[1:1 + n_in], refs[1 + n_in:1 + 2 * n_in]):
        acc = acc + _dot(a_ref[...], w_ref[...])
    o_ref[...] = acc


def _mm_res(res, acts, w, tm):
    n, d = res.shape
    tm = _row_tile(n, tm)
    n_in = len(acts)
    kw = w.shape[0] // n_in
    in_specs = [pl.BlockSpec((tm, d), lambda i: (i, 0))]
    in_specs += [pl.BlockSpec((tm, kw), lambda i: (i, 0)) for _ in acts]
    in_specs += [pl.BlockSpec((kw, d), lambda i, j=j: (j, 0)) for j in range(n_in)]
    return pl.pallas_call(
        functools.partial(_mm_res_kernel, n_in=n_in),
        grid=(n // tm,),
        in_specs=in_specs,
        out_specs=pl.BlockSpec((tm, d), lambda i: (i, 0)),
        out_shape=jax.ShapeDtypeStruct((n, d), F32),
        compiler_params=_params("parallel"),
        name="out_proj",
    )(res, *acts, *([w] * n_in))


def _ffn_prompt_kernel(x_ref, g_ref, wg_ref, wl_ref, cwg_ref, cwl_ref, cbg_ref, cbl_ref, wd_ref,
                       o_ref, tg_ref, tl_ref, hb_s, cg_s, cl_s, *, tm, tiles_per_seq):
    i = pl.program_id(0)
    j = pl.program_id(1)

    @pl.when(j == 0)
    def _():
        hb_s[...] = _rms(x_ref[...], g_ref[...]).astype(BF16)

    hb = hb_s[...]
    first = (i % tiles_per_seq) == 0

    def branch(w_ref, cw_ref, cb_ref, carry_s, tail_ref):
        u = _dot(hb, w_ref[...])

        @pl.when(first)
        def _():
            carry_s[j] = jnp.zeros((8, u.shape[1]), F32)

        prev8 = carry_s[j]
        cw = cw_ref[...]
        c = (cb_ref[...] + cw[0:1] * _shift_rows(u, prev8, 2) + cw[1:2] * _shift_rows(u, prev8, 1)
             + cw[2:3] * u)
        carry_s[j] = u[tm - 8:tm]
        tail_ref[0] = u[tm - 8:tm]
        return c

    cg = branch(wg_ref, cwg_ref, cbg_ref, cg_s, tg_ref)
    cl = branch(wl_ref, cwl_ref, cbl_ref, cl_s, tl_ref)
    act = (cg * jax.nn.sigmoid(cg) * cl).astype(BF16)
    contrib = _dot(act, wd_ref[...])

    @pl.when(j == 0)
    def _():
        o_ref[...] = x_ref[...] + contrib

    @pl.when(j > 0)
    def _():
        o_ref[...] += contrib


def _ffn_prompt(x, g, w_up, cw, cb, w_down, *, seq, tm, tf):
    n, d = x.shape
    tm = _row_tile(seq, tm)
    nf = D_FF // tf
    nt = n // tm
    return pl.pallas_call(
        functools.partial(_ffn_prompt_kernel, tm=tm, tiles_per_seq=seq // tm),
        grid=(nt, nf),
        in_specs=[pl.BlockSpec((tm, d), lambda i, j: (i, 0)), _const((1, d)),
                  pl.BlockSpec((d, tf), lambda i, j: (0, j)),
                  pl.BlockSpec((d, tf), lambda i, j: (0, j + nf)),
                  pl.BlockSpec((3, tf), lambda i, j: (0, j)),
                  pl.BlockSpec((3, tf), lambda i, j: (0, j + nf)),
                  pl.BlockSpec((1, tf), lambda i, j: (0, j)),
                  pl.BlockSpec((1, tf), lambda i, j: (0, j + nf)),
                  pl.BlockSpec((tf, d), lambda i, j: (j, 0))],
        out_specs=[pl.BlockSpec((tm, d), lambda i, j: (i, 0)),
                   pl.BlockSpec((1, 8, tf), lambda i, j: (i, 0, j)),
                   pl.BlockSpec((1, 8, tf), lambda i, j: (i, 0, j))],
        out_shape=[jax.ShapeDtypeStruct((n, d), F32),
                   jax.ShapeDtypeStruct((nt, 8, D_FF), F32),
                   jax.ShapeDtypeStruct((nt, 8, D_FF), F32)],
        scratch_shapes=[pltpu.VMEM((tm, d), BF16), pltpu.VMEM((nf, 8, tf), F32),
                        pltpu.VMEM((nf, 8, tf), F32)],
        compiler_params=_params("arbitrary", "arbitrary"),
        name="ffn_prompt",
    )(x, g, w_up, w_up, cw, cw, cb, cb, w_down)


def _ffn_sample_kernel(x_ref, g_ref, wg_ref, wl_ref, cwg_ref, cwl_ref, cbg_ref, cbl_ref, wd_ref,
                       hg_ref, hl_ref, o_ref, tg_ref, tl_ref, hb_s, *, db, steps):
    j = pl.program_id(0)
    n = steps * db

    @pl.when(j == 0)
    def _():
        hb_s[...] = _rms(x_ref[...], g_ref[...]).astype(BF16)

    hb = hb_s[...]

    def branch(w_ref, cw_ref, cb_ref, hist_ref, tail_ref):
        u = _dot(hb, w_ref[...])
        ext = jnp.concatenate([hist_ref[...], u], axis=0)
        cw = cw_ref[...]
        c = cb_ref[...] + cw[0:1] * ext[0:n] + cw[1:2] * ext[db:db + n] + cw[2:3] * ext[2 * db:2 * db + n]
        tail_ref[...] = ext[n:n + 2 * db]
        return c

    cg = branch(wg_ref, cwg_ref, cbg_ref, hg_ref, tg_ref)
    cl = branch(wl_ref, cwl_ref, cbl_ref, hl_ref, tl_ref)
    act = (cg * jax.nn.sigmoid(cg) * cl).astype(BF16)
    contrib = _dot(act, wd_ref[...])

    @pl.when(j == 0)
    def _():
        o_ref[...] = x_ref[...] + contrib

    @pl.when(j > 0)
    def _():
        o_ref[...] += contrib


def _ffn_sample(x, g, w_up, cw, cb, w_down, hist, *, db, steps, tf):
    n, d = x.shape
    nf = D_FF // tf
    return pl.pallas_call(
        functools.partial(_ffn_sample_kernel, db=db, steps=steps),
        grid=(nf,),
        in_specs=[_const((n, d)), _const((1, d)),
                  pl.BlockSpec((d, tf), lambda j: (0, j)),
                  pl.BlockSpec((d, tf), lambda j: (0, j + nf)),
                  pl.BlockSpec((3, tf), lambda j: (0, j)),
                  pl.BlockSpec((3, tf), lambda j: (0, j + nf)),
                  pl.BlockSpec((1, tf), lambda j: (0, j)),
                  pl.BlockSpec((1, tf), lambda j: (0, j + nf)),
                  pl.BlockSpec((tf, d), lambda j: (j, 0)),
                  pl.BlockSpec((2 * db, tf), lambda j: (0, j)),
                  pl.BlockSpec((2 * db, tf), lambda j: (0, j + nf))],
        out_specs=[_const((n, d)),
                   pl.BlockSpec((2 * db, tf), lambda j: (0, j)),
                   pl.BlockSpec((2 * db, tf), lambda j: (0, j))],
        out_shape=[jax.ShapeDtypeStruct((n, d), F32),
                   jax.ShapeDtypeStruct((2 * db, D_FF), F32),
                   jax.ShapeDtypeStruct((2 * db, D_FF), F32)],
        scratch_shapes=[pltpu.VMEM((n, d), BF16)],
        compiler_params=_params("arbitrary"),
        name="ffn_sample",
    )(x, g, w_up, w_up, cw, cw, cb, cb, w_down, hist, hist)


def _mla_proj_kernel(x_ref, g_ref, win_ref, qag_ref, kvg_ref, rkg_ref, wq_ref, gq_ref, gm_ref,
                     wk_ref, gk_ref, wv_ref, cq_ref, sq_ref, ck_ref, sk_ref,
                     qh_ref, kh_ref, v_ref, rows_ref, *, tm):
    hb = _rms(x_ref[...], g_ref[...]).astype(BF16)
    z = _dot(hb, win_ref[...])
    cq = _rms(z[:, 0:Q_RANK], qag_ref[...])
    ckv = _rms(z[:, Q_RANK:Q_RANK + KV_RANK], kvg_ref[...])
    zr = z[:, Q_RANK + KV_RANK:Q_RANK + KV_RANK + 128]
    kr = zr * lax.rsqrt(jnp.sum(zr * zr, axis=-1, keepdims=True) * (1.0 / ROPE_C) + EPS) * rkg_ref[...]
    lane = lax.broadcasted_iota(jnp.int32, (tm, 128), 1)
    half = ROPE_C // 2
    rot_k = jnp.where(lane < half, -pltpu.roll(kr, 128 - half, 1),
                      jnp.where(lane < ROPE_C, pltpu.roll(kr, half, 1), 0.0))
    krf = kr * ck_ref[...] + rot_k * sk_ref[...]
    rows_ref[:, 0:KV_RANK] = ckv
    rows_ref[:, KV_RANK:LAT_C] = krf[:, 0:ROPE_C]

    cqb = cq.astype(BF16)
    ckvb = ckv.astype(BF16)
    gm = gm_ref[...]
    kr_place = pltpu.roll(krf, NOPE_C, 1)
    kr2 = jnp.concatenate([kr_place, kr_place], axis=1)
    cos2 = jnp.concatenate([cq_ref[...], cq_ref[...]], axis=1)
    sin2 = jnp.concatenate([sq_ref[...], sq_ref[...]], axis=1)
    lane2 = lax.broadcasted_iota(jnp.int32, (tm, 256), 1) % 128
    first_half = jnp.logical_and(lane2 >= NOPE_C, lane2 < NOPE_C + half)
    second_half = jnp.logical_and(lane2 >= NOPE_C + half, lane2 < NOPE_C + ROPE_C)
    for c in range(H_C * HEAD_PAD // 256):
        cols = slice(c * 256, (c + 1) * 256)
        zq = _dot(cqb, wq_ref[:, cols])
        qn = zq * lax.rsqrt(_dot((zq * zq).astype(BF16), gm) + EPS) * gq_ref[:, cols]
        rot_q = jnp.where(first_half, -pltpu.roll(qn, 256 - half, 1),
                          jnp.where(second_half, pltpu.roll(qn, half, 1), 0.0))
        qh_ref[:, cols] = (qn * cos2 + rot_q * sin2).astype(BF16)
        zk = _dot(ckvb, wk_ref[:, cols])
        kn = zk * lax.rsqrt(_dot((zk * zk).astype(BF16), gm) + EPS) * gk_ref[:, cols] + kr2
        kh_ref[:, cols] = kn.astype(BF16)
    v_ref[...] = _dot(ckvb, wv_ref[...]).astype(BF16)


def _mla_proj(x, g, win, qag, kvg, rkg, wq, gq, gm, wk, gk, wv, tabs, *, tab_tiles, tm):
    n, d = x.shape
    tm = _row_tile(n, tm)
    hw = H_C * HEAD_PAD
    row = lambda c: pl.BlockSpec((tm, c), lambda i: (i, 0))
    tab = pl.BlockSpec((tm, 128), lambda i: (i % tab_tiles, 0))
    return pl.pallas_call(
        functools.partial(_mla_proj_kernel, tm=tm),
        grid=(n // tm,),
        in_specs=[row(d), _const((1, d)), _const((d, 512)), _const((1, Q_RANK)), _const((1, KV_RANK)),
                  _const((1, 128)), _const((Q_RANK, hw)), _const((1, hw)), _const((256, 256)),
                  _const((KV_RANK, hw)), _const((1, hw)), _const((KV_RANK, H_C * V_C)),
                  tab, tab, tab, tab],
        out_specs=[row(hw), row(hw), row(H_C * V_C), row(LAT_C)],
        out_shape=[jax.ShapeDtypeStruct((n, hw), BF16), jax.ShapeDtypeStruct((n, hw), BF16),
                   jax.ShapeDtypeStruct((n, H_C * V_C), BF16), jax.ShapeDtypeStruct((n, LAT_C), F32)],
        compiler_params=_params("parallel"),
        name="mla_proj",
    )(x, g, win, qag, kvg, rkg, wq, gq, gm, wk, gk, wv, *tabs)


def _diff_sample_kernel(pt_ref, qbd_ref, knew_ref, lam_ref, sg_ref, *refs, n_pages_step, n_steps, lam_init):
    del pt_ref
    page_refs = refs[:n_pages_step]
    o_ref, m_s, l_s, acc_s = refs[n_pages_step:]
    c = pl.program_id(1)

    @pl.when(c == 0)
    def _():
        m_s[...] = jnp.full_like(m_s, NEG_BIG)
        l_s[...] = jnp.zeros_like(l_s)
        acc_s[...] = jnp.zeros_like(acc_s)

    qf = qbd_ref[0]
    qb = qf.astype(BF16)
    s = jnp.concatenate([_nt_dot(qb, pr[0, :, 0:512].astype(BF16)) for pr in page_refs], axis=1)
    m_old = m_s[...]
    m_new = jnp.maximum(m_old, jnp.max(s, axis=-1, keepdims=True))
    alpha = jnp.exp(m_old - m_new)
    p = jnp.exp(s - m_new)
    l_s[...] = alpha * l_s[...] + jnp.sum(p, axis=-1, keepdims=True)
    pb = p.astype(BF16)
    pv = _dot(pb[:, 0:PAGE], page_refs[0][0, :, 512:1024].astype(BF16))
    for i in range(1, n_pages_step):
        pv = pv + _dot(pb[:, i * PAGE:(i + 1) * PAGE], page_refs[i][0, :, 512:1024].astype(BF16))
    acc_s[...] = alpha * acc_s[...] + pv
    m_s[...] = m_new

    @pl.when(c == n_steps - 1)
    def _():
        row = lax.broadcasted_iota(jnp.int32, (64, 1), 0)
        qidx = row % 8
        m0 = m_s[...]
        sj = []
        for j in range(4):
            kj = knew_ref[0, j:j + 1, 0:512]
            sj.append(jnp.where(j <= qidx, jnp.sum(qf * kj, axis=-1, keepdims=True), NEG_BIG))
        m1 = jnp.maximum(jnp.maximum(jnp.maximum(m0, sj[0]), jnp.maximum(sj[1], sj[2])), sj[3])
        al = jnp.exp(m0 - m1)
        l1 = al * l_s[...]
        acc = al * acc_s[...]
        for j in range(4):
            pj = jnp.exp(sj[j] - m1)
            l1 = l1 + pj
            acc = acc + pj * knew_ref[0, j:j + 1, 512:1024]
        accn = acc * (1.0 / l1)
        lv = lam_ref[...]
        lam = (jnp.exp(jnp.sum(lv[0:1] * lv[1:2], axis=-1, keepdims=True))
               - jnp.exp(jnp.sum(lv[2:3] * lv[3:4], axis=-1, keepdims=True)) + lam_init)
        o = accn[0:32] - lam * accn[32:64]
        lane_head = lax.broadcasted_iota(jnp.int32, (32, 512), 1) // 128
        row_head = lax.broadcasted_iota(jnp.int32, (32, 512), 0) // 8
        o = jnp.where(lane_head == row_head, o, 0.0)
        o = (o * lax.rsqrt(jnp.sum(o * o, axis=-1, keepdims=True) * (1.0 / 128.0) + EPS)
             * sg_ref[...] * (1.0 - lam_init))
        o_ref[0] = o[0:8] + o[8:16] + o[16:24] + o[24:32]


def _diff_sample(page_table, qbd, knew, lam, sg, cache, *, n_pages_step, lam_init):
    db, n_pages = page_table.shape
    n_steps = n_pages // n_pages_step
    page_specs = [pl.BlockSpec((1, PAGE, 1024), lambda s, c, pt, i=i: (pt[s, c * n_pages_step + i], 0, 0))
                  for i in range(n_pages_step)]
    grid_spec = pltpu.PrefetchScalarGridSpec(
        num_scalar_prefetch=1,
        grid=(db, n_steps),
        in_specs=[pl.BlockSpec((1, 64, 512), lambda s, c, pt: (s, 0, 0)),
                  pl.BlockSpec((1, 8, 1024), lambda s, c, pt: (s, 0, 0)),
                  pl.BlockSpec((4, HD_A), lambda s, c, pt: (0, 0)),
                  pl.BlockSpec((1, 512), lambda s, c, pt: (0, 0))] + page_specs,
        out_specs=pl.BlockSpec((1, 8, 512), lambda s, c, pt: (s, 0, 0)),
        scratch_shapes=[pltpu.VMEM((64, 1), F32), pltpu.VMEM((64, 1), F32), pltpu.VMEM((64, 512), F32)],
    )
    return pl.pallas_call(
        functools.partial(_diff_sample_kernel, n_pages_step=n_pages_step, n_steps=n_steps, lam_init=lam_init),
        grid_spec=grid_spec,
        out_shape=jax.ShapeDtypeStruct((db, 8, 512), F32),
        compiler_params=_params("arbitrary", "arbitrary"),
        name="diff_sample",
    )(page_table, qbd, knew, lam, sg, *([cache] * n_pages_step))


def _mla_sample_kernel(pt_ref, qbd_ref, qrope_ref, qrows_ref, knew_ref, cnew_ref, wkt_ref, wkg_ref, wv_ref,
                       *refs, n_pages_step, n_steps):
    del pt_ref
    page_refs = refs[:n_pages_step]
    o_ref, m_s, l_s, lat_s, qabs_s, kr_s = refs[n_pages_step:]
    c = pl.program_id(1)
    nk = n_pages_step * PAGE

    @pl.when(c == 0)
    def _():
        m_s[...] = jnp.full_like(m_s, NEG_BIG)
        l_s[...] = jnp.zeros_like(l_s)
        lat_s[...] = jnp.zeros_like(lat_s)
        qabs_s[...] = _dot(qbd_ref[0], wkg_ref[...]).astype(BF16)

    kr_s[...] = jnp.zeros_like(kr_s)
    for i, pr in enumerate(page_refs):
        kr_s[i * PAGE:(i + 1) * PAGE, 0:ROPE_C] = pr[0, :, KV_RANK:LAT_C]
    cb = jnp.concatenate([pr[0, :, 0:KV_RANK].astype(BF16) for pr in page_refs], axis=0)
    krb = kr_s[...].astype(BF16)
    kpt = _nt_dot(wkt_ref[...], cb)
    msq = jnp.sum((kpt * kpt).reshape(H_C, NOPE_C, nk), axis=1) * (1.0 / NOPE_C)
    rinv = lax.rsqrt(msq + EPS)
    s = (_nt_dot(qabs_s[...], cb) * jnp.concatenate([rinv] * 4, axis=0)
         + _nt_dot(qrope_ref[0], krb))
    m_old = m_s[...]
    m_new = jnp.maximum(m_old, jnp.max(s, axis=-1, keepdims=True))
    alpha = jnp.exp(m_old - m_new)
    p = jnp.exp(s - m_new)
    l_s[...] = alpha * l_s[...] + jnp.sum(p, axis=-1, keepdims=True)
    lat_s[...] = alpha * lat_s[...] + _dot(p.astype(BF16), cb)
    m_s[...] = m_new

    @pl.when(c == n_steps - 1)
    def _():
        qrows = qrows_ref[0]
        qidx = lax.broadcasted_iota(jnp.int32, (64, 1), 0) // H_C
        m0 = m_s[...]
        sj = []
        for j in range(4):
            kt = jnp.concatenate([knew_ref[0, j]] * 4, axis=0)
            sj.append(jnp.where(j <= qidx, jnp.sum(qrows * kt, axis=-1, keepdims=True), NEG_BIG))
        m1 = jnp.maximum(jnp.maximum(jnp.maximum(m0, sj[0]), jnp.maximum(sj[1], sj[2])), sj[3])
        al = jnp.exp(m0 - m1)
        l1 = al * l_s[...]
        lat = al * lat_s[...]
        for j in range(4):
            pj = jnp.exp(sj[j] - m1)
            l1 = l1 + pj
            lat = lat + pj * cnew_ref[0, j:j + 1, :]
        latn = (lat * (1.0 / l1)).astype(BF16)
        full = _dot(latn, wv_ref[...])
        lane_head = lax.broadcasted_iota(jnp.int32, (64, H_C * V_C), 1) // V_C
        row_head = lax.broadcasted_iota(jnp.int32, (64, H_C * V_C), 0) % H_C
        full = jnp.where(lane_head == row_head, full, 0.0)
        for q in range(4):
            o_ref[0, q:q + 1, :] = jnp.sum(full[q * H_C:(q + 1) * H_C], axis=0, keepdims=True)
        o_ref[0, 4:8, :] = jnp.zeros((4, H_C * V_C), F32)


def _mla_sample(page_table, qbd, qrope, qrows, knew, cnew, wkt, wkg, wv, cache, *, n_pages_step):
    db, n_pages = page_table.shape
    n_steps = n_pages // n_pages_step
    nk = n_pages_step * PAGE
    seq3 = lambda a, b: pl.BlockSpec((1, a, b), lambda s, c, pt: (s, 0, 0))
    cst = lambda a, b: pl.BlockSpec((a, b), lambda s, c, pt: (0, 0))
    page_specs = [pl.BlockSpec((1, PAGE, LAT_C), lambda s, c, pt, i=i: (pt[s, c * n_pages_step + i], 0, 0))
                  for i in range(n_pages_step)]
    grid_spec = pltpu.PrefetchScalarGridSpec(
        num_scalar_prefetch=1,
        grid=(db, n_steps),
        in_specs=[seq3(64, H_C * NOPE_C), seq3(64, 128), seq3(64, 128),
                  pl.BlockSpec((1, 4, H_C, 128), lambda s, c, pt: (s, 0, 0, 0)),
                  seq3(8, KV_RANK), cst(H_C * NOPE_C, KV_RANK), cst(H_C * NOPE_C, KV_RANK),
                  cst(KV_RANK, H_C * V_C)] + page_specs,
        out_specs=seq3(8, H_C * V_C),
        scratch_shapes=[pltpu.VMEM((64, 1), F32), pltpu.VMEM((64, 1), F32), pltpu.VMEM((64, KV_RANK), F32),
                        pltpu.VMEM((64, KV_RANK), BF16), pltpu.VMEM((nk, 128), F32)],
    )
    return pl.pallas_call(
        functools.partial(_mla_sample_kernel, n_pages_step=n_pages_step, n_steps=n_steps),
        grid_spec=grid_spec,
        out_shape=jax.ShapeDtypeStruct((db, 8, H_C * V_C), F32),
        compiler_params=_params("arbitrary", "arbitrary"),
        name="mla_sample",
    )(page_table, qbd, qrope, qrows, knew, cnew, wkt, wkg, wv, *([cache] * n_pages_step))


def _group_mean_matrix(sizes, total):
    m = jnp.zeros((total, total), F32)
    off = 0
    for sz, live in sizes:
        if live:
            m = m.at[off:off + sz, off:off + sz].set(1.0 / sz)
        off += sz
    return m.astype(BF16)


def _rope_tables(pos, lane_offset):
    half = ROPE_C // 2
    inv = ROPE_THETA ** (-jnp.arange(half, dtype=F32) / half)
    ang = pos.astype(F32)[:, None] * inv[None, :]
    c2 = jnp.concatenate([jnp.cos(ang), jnp.cos(ang)], axis=1)
    s2 = jnp.concatenate([jnp.sin(ang), jnp.sin(ang)], axis=1)
    cos = jnp.ones((pos.shape[0], 128), F32).at[:, lane_offset:lane_offset + ROPE_C].set(c2)
    sin = jnp.zeros((pos.shape[0], 128), F32).at[:, lane_offset:lane_offset + ROPE_C].set(s2)
    return cos, sin


def kernel(x_prompt, x_sample, cache_kv_diff, cache_mla, state_lru_h, state_lru_conv, state_ffn_conv, page_table,
           norm_mix, norm_ffn, w_in_e, qk_norm_a, lambda_a, subln_a, conv_w_lru, conv_b_lru, gate_w_lru,
           gate_b_lru, lambda_lru, w_out_e, w_in_c, q_a_norm, kv_a_norm, w_uq, w_ukv, qn_c, rn_c, w_out_c,
           w_up, conv_w_ffn, conv_b_ffn, w_down):
    B, T, D = x_prompt.shape
    DB, S, _ = x_sample.shape
    n_pages = page_table.shape[1]
    n_past = n_pages * PAGE
    n_pool = cache_kv_diff.shape[1]
    depth = norm_mix.shape[0]
    NP, NS = B * T, S * DB
    pt = page_table.astype(jnp.int32)

    xp = x_prompt.reshape(NP, D)
    xs = jnp.swapaxes(x_sample, 0, 1).reshape(NS, D)

    def to_seq_major(a):
        return jnp.swapaxes(a.reshape((S, DB) + a.shape[1:]), 0, 1)

    pos_p = jnp.arange(T, dtype=jnp.int32)
    pos_s = jnp.repeat(n_past + jnp.arange(S, dtype=jnp.int32), DB)
    tabs_p = _rope_tables(pos_p, NOPE_C) + _rope_tables(pos_p, 0)
    tabs_s = _rope_tables(pos_s, NOPE_C) + _rope_tables(pos_s, 0)

    kv_p, kv_s, mla_p, mla_s = [], [], [], []
    lh_p, lh_s, lc_p, lc_s, fc_p, fc_s = [], [], [], [], [], []

    for l in range(depth):
        g_mix = norm_mix[l].reshape(1, D)
        if l % 2 == 0:
            e = l // 2
            lam_init = 0.8 - 0.6 * math.exp(-0.3 * l)
            w_in = w_in_e[e].astype(BF16)
            qs = jnp.tile(qk_norm_a[e, 0].reshape(1, 2 * HD_A), (1, H_A)) * (1.0 / math.sqrt(HD_A))
            ks = jnp.tile(qk_norm_a[e, 1].reshape(1, 2 * HD_A), (1, H_A))
            gm64 = _group_mean_matrix([(HD_A, True)] * (2 * H_A), 2 * H_A * HD_A)
            lam_v = lambda_a[e]
            sg = subln_a[e].reshape(1, 2 * HD_A)
            cw, cb = conv_w_lru[e], conv_b_lru[e].reshape(1, D_RNN)
            gw = jnp.zeros((2, 8, 64, 8, 64), F32)
            gw = gw.at[:, jnp.arange(8), :, jnp.arange(8), :].set(jnp.swapaxes(gate_w_lru[e], 0, 1))
            gw = jnp.concatenate([gw[0].reshape(D_RNN, D_RNN), gw[1].reshape(D_RNN, D_RNN)], axis=1).astype(BF16)
            gb = gate_b_lru[e].reshape(1, 2 * D_RNN)
            lam_lru = lambda_lru[e].reshape(1, D_RNN)
            w_out = w_out_e[e].astype(BF16)

            qb_p, kb_p, vb_p, kvf_p, xg_p = _even_proj(xp, g_mix, w_in, qs, ks, gm64, 512)
            qb_s, kb_s, vb_s, kvf_s, xg_s = _even_proj(xs, g_mix, w_in, qs, ks, gm64, 512)

            a_p = _flash(qb_p, kb_p, vb_p, (lam_v, sg), diff=True, batch=B, seq=T, tb=256, lam_init=lam_init)

            q_sm = to_seq_major(qb_s.astype(F32)).reshape(DB, S, H_A, 2, HD_A)
            eye_h = jnp.eye(H_A, dtype=F32)
            eye_m = jnp.eye(2, dtype=F32)
            qbd = jnp.einsum('bqhmd,hg,mn->bmhqgnd', q_sm, eye_h, eye_m)
            qbd = jnp.pad(qbd, ((0, 0), (0, 0), (0, 0), (0, 8 - S), (0, 0), (0, 0), (0, 0)))
            qbd = qbd.reshape(DB, 2 * H_A * 8, H_A * 2 * HD_A)
            knew = jnp.pad(to_seq_major(kvf_s), ((0, 0), (0, 8 - S), (0, 0)))
            cache_d = cache_kv_diff.reshape(cache_kv_diff.shape[0] * n_pool, PAGE, 1024)
            sg4 = jnp.tile(sg, (1, H_A))
            a_s = _diff_sample(pt + e * n_pool, qbd, knew, lam_v, sg4, cache_d,
                               n_pages_step=min(16, n_pages), lam_init=lam_init)
            a_s = jnp.swapaxes(a_s[:, :S], 0, 1).reshape(NS, H_A * 2 * HD_A).astype(BF16)

            y_p, hl_pp, ct_pp = _lru_prompt(xg_p, cw, cb, gw, gb, lam_lru, batch=B, seq=T, tm=256)
            hist = jnp.swapaxes(state_lru_conv[e], 0, 1).reshape(3 * DB, D_RNN)
            y_s, hl_ss, ct_ss = _lru_sample(xg_s, hist, state_lru_h[e], cw, cb, gw, gb, lam_lru, db=DB, steps=S)

            xp = _mm_res(xp, [a_p, y_p], w_out, 512)
            xs = _mm_res(xs, [a_s, y_s], w_out, 512)

            kv_p.append(kvf_p.reshape(B, T, 2, H_A, 2 * HD_A))
            kv_s.append(to_seq_major(kvf_s).reshape(DB, S, 2, H_A, 2 * HD_A))
            nt = hl_pp.shape[0] // B
            lh_p.append(hl_pp.reshape(B, nt, 8, D_RNN)[:, -1, -1])
            lc_p.append(ct_pp.reshape(B, nt, 8, D_RNN)[:, -1, 5:8])
            lh_s.append(hl_ss)
            lc_s.append(jnp.swapaxes(ct_ss.reshape(3, DB, D_RNN), 0, 1))
        else:
            o = l // 2
            scale = 1.0 / math.sqrt(NOPE_C + ROPE_C)
            pad_h = HEAD_PAD - NOPE_C - ROPE_C
            win = jnp.pad(w_in_c[o], ((0, 0), (0, 512 - w_in_c.shape[2]))).astype(BF16)
            qag = q_a_norm[o].reshape(1, Q_RANK)
            kvg = kv_a_norm[o].reshape(1, KV_RANK)
            rkg = jnp.pad(rn_c[o, 1], (0, 128 - ROPE_C)).reshape(1, 128)
            wq = jnp.pad(w_uq[o], ((0, 0), (0, 0), (0, pad_h))).reshape(Q_RANK, H_C * HEAD_PAD).astype(BF16)
            gq = jnp.tile(jnp.concatenate([qn_c[o, 0], rn_c[o, 0], jnp.zeros((pad_h,), F32)]) * scale,
                          H_C).reshape(1, H_C * HEAD_PAD)
            gm256 = _group_mean_matrix([(NOPE_C, True), (ROPE_C, True), (pad_h, False)] * 2, 256)
            wk_nope = w_ukv[o][:, :, :NOPE_C]
            wk = jnp.pad(wk_nope, ((0, 0), (0, 0), (0, HEAD_PAD - NOPE_C))).reshape(KV_RANK, H_C * HEAD_PAD)
            wk = wk.astype(BF16)
            gk = jnp.tile(jnp.concatenate([qn_c[o, 1], jnp.zeros((HEAD_PAD - NOPE_C,), F32)]),
                          H_C).reshape(1, H_C * HEAD_PAD)
            wv = w_ukv[o][:, :, NOPE_C:].reshape(KV_RANK, H_C * V_C).astype(BF16)
            wkt = jnp.transpose(wk_nope.reshape(KV_RANK, H_C * NOPE_C))
            wkg = (wkt * jnp.tile(qn_c[o, 1], H_C)[:, None]).astype(BF16)
            wkt = wkt.astype(BF16)
            w_out = w_out_c[o].astype(BF16)

            tm_p = _row_tile(T, 512)
            qh_p, kh_p, v_p, rows_p = _mla_proj(xp, g_mix, win, qag, kvg, rkg, wq, gq, gm256, wk, gk, wv, tabs_p,
                                                tab_tiles=T // tm_p, tm=tm_p)
            qh_s, kh_s, v_s, rows_s = _mla_proj(xs, g_mix, win, qag, kvg, rkg, wq, gq, gm256, wk, gk, wv, tabs_s,
                                                tab_tiles=1, tm=NS)

            o_p = _flash(qh_p, kh_p, v_p, (), diff=False, batch=B, seq=T, tb=256)

            qh_sm = to_seq_major(qh_s.astype(F32)).reshape(DB, S, H_C, HEAD_PAD)
            qrows = qh_sm.reshape(DB, S * H_C, HEAD_PAD)
            qn = qh_sm[..., :NOPE_C]
            qbd = jnp.einsum('bqhd,hg->bqhgd', qn, jnp.eye(H_C, dtype=F32)).reshape(DB, S * H_C, H_C * NOPE_C)
            qrope = jnp.pad(qh_sm[..., NOPE_C:NOPE_C + ROPE_C],
                            ((0, 0), (0, 0), (0, 0), (0, 128 - ROPE_C))).reshape(DB, S * H_C, 128)
            knew = to_seq_major(kh_s.astype(F32)).reshape(DB, S, H_C, HEAD_PAD)
            rows_sm = to_seq_major(rows_s)
            cnew = jnp.pad(rows_sm[..., :KV_RANK], ((0, 0), (0, 8 - S), (0, 0)))
            cache_m = cache_mla.reshape(cache_mla.shape[0] * n_pool, PAGE, LAT_C)
            o_s = _mla_sample(pt + o * n_pool, qbd.astype(BF16), qrope.astype(BF16), qrows, knew, cnew,
                              wkt, wkg, wv, cache_m, n_pages_step=min(8, n_pages))
            o_s = jnp.swapaxes(o_s[:, :S], 0, 1).reshape(NS, H_C * V_C).astype(BF16)

            xp = _mm_res(xp, [o_p], w_out, 512)
            xs = _mm_res(xs, [o_s], w_out, 512)
            mla_p.append(rows_p.reshape(B, T, LAT_C))
            mla_s.append(rows_sm)

        g_ffn = norm_ffn[l].reshape(1, D)
        wu = w_up[l].astype(BF16)
        wd = w_down[l].astype(BF16)
        cwf, cbf = conv_w_ffn[l], conv_b_ffn[l].reshape(1, 2 * D_FF)
        xp, tg, tl = _ffn_prompt(xp, g_ffn, wu, cwf, cbf, wd, seq=T, tm=512, tf=1408)
        ntf = tg.shape[0] // B
        tail = jnp.concatenate([tg, tl], axis=-1).reshape(B, ntf, 8, 2 * D_FF)
        fc_p.append(tail[:, -1, 6:8])
        hist_f = jnp.swapaxes(state_ffn_conv[l], 0, 1).reshape(2 * DB, 2 * D_FF)
        xs, tgs, tls = _ffn_sample(xs, g_ffn, wu, cwf, cbf, wd, hist_f, db=DB, steps=S, tf=1408)
        fc_s.append(jnp.swapaxes(jnp.concatenate([tgs, tls], axis=-1).reshape(2, DB, 2 * D_FF), 0, 1))

    y_p = xp.reshape(B, T, D)
    y_s = to_seq_major(xs)
    return (y_p, y_s, jnp.stack(kv_p), jnp.stack(kv_s), jnp.stack(mla_p), jnp.stack(mla_s),
            jnp.stack(lh_p), jnp.stack(lh_s), jnp.stack(lc_p), jnp.stack(lc_s), jnp.stack(fc_p), jnp.stack(fc_s))
```

```python
import functools
import math

import jax
import jax.numpy as jnp
from jax import lax
from jax.experimental import pallas as pl
from jax.experimental.pallas import tpu as pltpu

F32 = jnp.float32
BF16 = jnp.bfloat16

EPS = 1e-6
NEG_BIG = -1e30
ROPE_THETA = 10000.0
LRU_C = 8.0
PAGE = 128

H_A = 4
HD_A = 64
D_RNN = 512
H_C = 16
Q_RANK = 256
KV_RANK = 128
NOPE_C = 64
ROPE_C = 32
V_C = 64
LAT_C = KV_RANK + ROPE_C
HEAD_PAD = 128
D_FF = 2816

VMEM_LIMIT_BYTES = 56 * 1024 * 1024


def _params(*semantics):
    return pltpu.CompilerParams(dimension_semantics=semantics,
                                vmem_limit_bytes=VMEM_LIMIT_BYTES)


def _row_tile(n, want):
    t = min(n, want)
    assert n % t == 0
    return t


def _const(shape):
    return pl.BlockSpec(shape, lambda *_: (0,) * len(shape))


def _rms(x, g):
    return x * lax.rsqrt(jnp.mean(x * x, axis=-1, keepdims=True) + EPS) * g


def _nt_dot(a, b):
    return lax.dot_general(a, b, (((1,), (1,)), ((), ())), preferred_element_type=F32)


def _dot(a, b):
    return jnp.dot(a, b, preferred_element_type=F32)


def _gelu_tanh(x):
    return 0.5 * x * (1.0 + jnp.tanh(math.sqrt(2.0 / math.pi) * (x + 0.044715 * x * x * x)))


def _softplus(y):
    return jnp.maximum(y, 0.0) + jnp.log1p(jnp.exp(-jnp.abs(y)))


def _shift_rows(x, prev8, s):
    xr = pltpu.roll(x, s, 0)
    row8 = lax.broadcasted_iota(jnp.int32, prev8.shape, 0)
    top = jnp.where(row8 < s, pltpu.roll(prev8, s, 0), xr[0:8])
    return jnp.concatenate([top, xr[8:]], axis=0)


def _even_proj_kernel(x_ref, g_ref, w_ref, qs_ref, ks_ref, gm_ref,
                      qb_ref, kb_ref, vb_ref, kv_ref, xg_ref):
    hb = _rms(x_ref[...], g_ref[...]).astype(BF16)
    z = _dot(hb, w_ref[...])
    zq, zk, zv = z[:, 0:512], z[:, 512:1024], z[:, 1024:1536]
    gm = gm_ref[...]
    mq = _dot((zq * zq).astype(BF16), gm)
    mk = _dot((zk * zk).astype(BF16), gm)
    q = zq * lax.rsqrt(mq + EPS) * qs_ref[...]
    k = zk * lax.rsqrt(mk + EPS) * ks_ref[...]
    qb_ref[...] = q.astype(BF16)
    kb_ref[...] = k.astype(BF16)
    vb_ref[...] = zv.astype(BF16)
    kv_ref[:, 0:512] = k
    kv_ref[:, 512:1024] = zv
    xg_ref[...] = z[:, 1536:2560]


def _even_proj(x, g, w, qs, ks, gm, tm):
    n = x.shape[0]
    tm = _row_tile(n, tm)
    row = lambda c: pl.BlockSpec((tm, c), lambda i: (i, 0))
    return pl.pallas_call(
        _even_proj_kernel,
        grid=(n // tm,),
        in_specs=[row(1024), _const((1, 1024)), _const((1024, 2560)), _const((1, 512)),
                  _const((1, 512)), _const((512, 512))],
        out_specs=[row(512), row(512), row(512), row(1024), row(1024)],
        out_shape=[jax.ShapeDtypeStruct((n, 512), BF16), jax.ShapeDtypeStruct((n, 512), BF16),
                   jax.ShapeDtypeStruct((n, 512), BF16), jax.ShapeDtypeStruct((n, 1024), F32),
                   jax.ShapeDtypeStruct((n, 1024), F32)],
        compiler_params=_params("parallel"),
        name="even_proj",
    )(x, g, w, qs, ks, gm)


def _online_update(s, m, acc, vext):
    m_new = jnp.maximum(m, jnp.max(s, axis=-1, keepdims=True))
    alpha = jnp.exp(m - m_new)
    p = jnp.exp(s - m_new).astype(BF16)
    return m_new, alpha * acc + _dot(p, vext)


def _flash_kernel(*refs, diff, tb, lam_init):
    if diff:
        q_ref, k_ref, v_ref, lam_ref, sg_ref, o_ref = refs
    else:
        q_ref, k_ref, v_ref, o_ref = refs
    qi = pl.program_id(2)
    q = q_ref[...]
    lane = lax.broadcasted_iota(jnp.int32, (tb, 128), 1)
    if diff:
        zero = jnp.zeros_like(q)
        qa = jnp.where(lane < HD_A, q, zero)
        qb = jnp.where(lane >= HD_A, q, zero)
    else:
        qa, qb = q[:, 0:128], q[:, 128:256]

    ones = jnp.ones((tb, 128), BF16)

    def kv_block(j):
        start = pl.multiple_of(j * tb, tb)
        kblk = k_ref[pl.ds(start, tb), :]
        vext = jnp.concatenate([v_ref[pl.ds(start, tb), :], ones], axis=1)
        if diff:
            return kblk, kblk, vext
        return kblk[:, 0:128], kblk[:, 128:256], vext

    def body(j, carry):
        m1, a1, m2, a2 = carry
        ka, kb, vext = kv_block(j)
        m1, a1 = _online_update(_nt_dot(qa, ka), m1, a1, vext)
        m2, a2 = _online_update(_nt_dot(qb, kb), m2, a2, vext)
        return m1, a1, m2, a2

    col1 = jnp.full((tb, 1), NEG_BIG, F32)
    zacc = jnp.zeros((tb, 256), F32)
    m1, a1, m2, a2 = lax.fori_loop(0, qi, body, (col1, zacc, col1, zacc))

    ka, kb, vext = kv_block(qi)
    r = lax.broadcasted_iota(jnp.int32, (tb, tb), 0)
    c = lax.broadcasted_iota(jnp.int32, (tb, tb), 1)
    keep = c <= r
    m1, a1 = _online_update(jnp.where(keep, _nt_dot(qa, ka), NEG_BIG), m1, a1, vext)
    m2, a2 = _online_update(jnp.where(keep, _nt_dot(qb, kb), NEG_BIG), m2, a2, vext)

    o1 = a1[:, 0:128] * (1.0 / a1[:, 128:129])
    o2 = a2[:, 0:128] * (1.0 / a2[:, 128:129])
    if diff:
        lv = lam_ref[...]
        lam = (jnp.exp(jnp.sum(lv[0:1] * lv[1:2], axis=-1, keepdims=True))
               - jnp.exp(jnp.sum(lv[2:3] * lv[3:4], axis=-1, keepdims=True)) + lam_init)
        o = o1 - lam * o2
        o = _rms(o, sg_ref[...]) * (1.0 - lam_init)
    else:
        o = jnp.where(lane < V_C, o1, o2)
    o_ref[...] = o.astype(o_ref.dtype)


def _flash(q, k, v, extra, *, diff, batch, seq, tb, lam_init=0.0):
    n = q.shape[0]
    tb = _row_tile(seq, tb)
    nq = seq // tb
    qw = 128 if diff else 256
    groups = q.shape[1] // qw
    in_specs = [pl.BlockSpec((tb, qw), lambda b, g, i: (b * nq + i, g)),
                pl.BlockSpec((seq, qw), lambda b, g, i: (b, g)),
                pl.BlockSpec((seq, 128), lambda b, g, i: (b, g))]
    in_specs += [_const(e.shape) for e in extra]
    return pl.pallas_call(
        functools.partial(_flash_kernel, diff=diff, tb=tb, lam_init=lam_init),
        grid=(batch, groups, nq),
        in_specs=in_specs,
        out_specs=pl.BlockSpec((tb, 128), lambda b, g, i: (b * nq + i, g)),
        out_shape=jax.ShapeDtypeStruct((n, groups * 128), BF16),
        compiler_params=_params("parallel", "parallel", "arbitrary"),
        name="flash_diff" if diff else "flash_mla",
    )(q, k, v, *extra)


def _lru_gates(xc, gw_ref, gb_ref, lam_ref):
    g = _dot(xc.astype(BF16), gw_ref[...]) + gb_ref[...]
    r = jax.nn.sigmoid(g[:, 0:D_RNN])
    i = jax.nn.sigmoid(g[:, D_RNN:2 * D_RNN])
    a = jnp.exp(-LRU_C * r * _softplus(-lam_ref[...]))
    mult = jnp.sqrt(1.0 - a * a)
    return a, mult, i


def _lru_prompt_kernel(x_ref, gr_ref, cw_ref, cb_ref, gw_ref, gb_ref, lam_ref,
                       y_ref, hl_ref, ct_ref, xprev_s, h_s, *, tm):
    ti = pl.program_id(1)

    @pl.when(ti == 0)
    def _():
        xprev_s[...] = jnp.zeros_like(xprev_s)
        h_s[...] = jnp.zeros_like(h_s)

    x = x_ref[...]
    prev8 = xprev_s[...]
    cw = cw_ref[...]
    xc = (cb_ref[...] + cw[0:1] * _shift_rows(x, prev8, 3) + cw[1:2] * _shift_rows(x, prev8, 2)
          + cw[2:3] * _shift_rows(x, prev8, 1) + cw[3:4] * x)
    a, mult, i = _lru_gates(xc, gw_ref, gb_ref, lam_ref)
    row = lax.broadcasted_iota(jnp.int32, (tm, D_RNN), 0)
    mult = jnp.where(jnp.logical_and(row == 0, ti == 0), 1.0, mult)
    b = mult * i * xc
    d = 1
    while d < tm:
        ar = jnp.where(row >= d, pltpu.roll(a, d, 0), 1.0)
        br = jnp.where(row >= d, pltpu.roll(b, d, 0), 0.0)
        b = a * br + b
        a = a * ar
        d *= 2
    h = a * h_s[7:8] + b
    y_ref[...] = (h * _gelu_tanh(gr_ref[...])).astype(BF16)
    h_s[...] = h[tm - 8:tm]
    hl_ref[0] = h[tm - 8:tm]
    ct_ref[0] = x[tm - 8:tm]
    xprev_s[...] = x[tm - 8:tm]


def _lru_prompt(xg, cw, cb, gw, gb, lam, *, batch, seq, tm):
    n = xg.shape[0]
    tm = _row_tile(seq, tm)
    nt = seq // tm
    return pl.pallas_call(
        functools.partial(_lru_prompt_kernel, tm=tm),
        grid=(batch, nt),
        in_specs=[pl.BlockSpec((tm, D_RNN), lambda b, t: (b * nt + t, 0)),
                  pl.BlockSpec((tm, D_RNN), lambda b, t: (b * nt + t, 1)),
                  _const((4, D_RNN)), _const((1, D_RNN)), _const((D_RNN, 2 * D_RNN)),
                  _const((1, 2 * D_RNN)), _const((1, D_RNN))],
        out_specs=[pl.BlockSpec((tm, D_RNN), lambda b, t: (b * nt + t, 0)),
                   pl.BlockSpec((1, 8, D_RNN), lambda b, t: (b * nt + t, 0, 0)),
                   pl.BlockSpec((1, 8, D_RNN), lambda b, t: (b * nt + t, 0, 0))],
        out_shape=[jax.ShapeDtypeStruct((n, D_RNN), BF16),
                   jax.ShapeDtypeStruct((batch * nt, 8, D_RNN), F32),
                   jax.ShapeDtypeStruct((batch * nt, 8, D_RNN), F32)],
        scratch_shapes=[pltpu.VMEM((8, D_RNN), F32), pltpu.VMEM((8, D_RNN), F32)],
        compiler_params=_params("arbitrary", "arbitrary"),
        name="lru_prompt",
    )(xg, xg, cw, cb, gw, gb, lam)


def _lru_sample_kernel(x_ref, gr_ref, hist_ref, h0_ref, cw_ref, cb_ref, gw_ref, gb_ref, lam_ref,
                       y_ref, hl_ref, ct_ref, *, db, steps):
    n = steps * db
    ext = jnp.concatenate([hist_ref[...], x_ref[...]], axis=0)
    cw = cw_ref[...]
    xc = (cb_ref[...] + cw[0:1] * ext[0:n] + cw[1:2] * ext[db:db + n]
          + cw[2:3] * ext[2 * db:2 * db + n] + cw[3:4] * ext[3 * db:3 * db + n])
    a, mult, i = _lru_gates(xc, gw_ref, gb_ref, lam_ref)
    b = mult * i * xc
    h = h0_ref[...]
    hs = []
    for t in range(steps):
        h = a[t * db:(t + 1) * db] * h + b[t * db:(t + 1) * db]
        hs.append(h)
    hseq = jnp.concatenate(hs, axis=0)
    y_ref[...] = (hseq * _gelu_tanh(gr_ref[...])).astype(BF16)
    hl_ref[...] = h
    ct_ref[...] = ext[n:n + 3 * db]


def _lru_sample(xg, hist, h0, cw, cb, gw, gb, lam, *, db, steps):
    n = steps * db
    return pl.pallas_call(
        functools.partial(_lru_sample_kernel, db=db, steps=steps),
        grid=(1,),
        in_specs=[pl.BlockSpec((n, D_RNN), lambda i: (0, 0)),
                  pl.BlockSpec((n, D_RNN), lambda i: (0, 1)),
                  _const((3 * db, D_RNN)), _const((db, D_RNN)),
                  _const((4, D_RNN)), _const((1, D_RNN)), _const((D_RNN, 2 * D_RNN)),
                  _const((1, 2 * D_RNN)), _const((1, D_RNN))],
        out_specs=[_const((n, D_RNN)), _const((db, D_RNN)), _const((3 * db, D_RNN))],
        out_shape=[jax.ShapeDtypeStruct((n, D_RNN), BF16),
                   jax.ShapeDtypeStruct((db, D_RNN), F32),
                   jax.ShapeDtypeStruct((3 * db, D_RNN), F32)],
        compiler_params=_params("arbitrary"),
        name="lru_sample",
    )(xg, xg, hist, h0, cw, cb, gw, gb, lam)


def _mm_res_kernel(*refs, n_in):
    res_ref, o_ref = refs[0], refs[-1]
    acc = res_ref[...]
    for a_ref, w_ref in zip(refs[1:1 + n_in], refs[1 + n_in:1 + 2 * n_in]):
        acc = acc + _dot(a_ref[...], w_ref[...])
    o_ref[...] = acc


def _mm_res(res, acts, w, tm):
    n, d = res.shape
    tm = _row_tile(n, tm)
    n_in = len(acts)
    kw = w.shape[0] // n_in
    in_specs = [pl.BlockSpec((tm, d), lambda i: (i, 0))]
    in_specs += [pl.BlockSpec((tm, kw), lambda i: (i, 0)) for _ in acts]
    in_specs += [pl.BlockSpec((kw, d), lambda i, j=j: (j, 0)) for j in range(n_in)]
    return pl.pallas_call(
        functools.partial(_mm_res_kernel, n_in=n_in),
        grid=(n // tm,),
        in_specs=in_specs,
        out_specs=pl.BlockSpec((tm, d), lambda i: (i, 0)),
        out_shape=jax.ShapeDtypeStruct((n, d), F32),
        compiler_params=_params("parallel"),
        name="out_proj",
    )(res, *acts, *([w] * n_in))


def _ffn_prompt_kernel(x_ref, g_ref, wg_ref, wl_ref, cwg_ref, cwl_ref, cbg_ref, cbl_ref, wd_ref,
                       o_ref, tg_ref, tl_ref, hb_s, cg_s, cl_s, *, tm, tiles_per_seq):
    i = pl.program_id(0)
    j = pl.program_id(1)

    @pl.when(j == 0)
    def _():
        hb_s[...] = _rms(x_ref[...], g_ref[...]).astype(BF16)

    hb = hb_s[...]
    first = (i % tiles_per_seq) == 0

    def branch(w_ref, cw_ref, cb_ref, carry_s, tail_ref):
        u = _dot(hb, w_ref[...])

        @pl.when(first)
        def _():
            carry_s[j] = jnp.zeros((8, u.shape[1]), F32)

        prev8 = carry_s[j]
        cw = cw_ref[...]
        c = (cb_ref[...] + cw[0:1] * _shift_rows(u, prev8, 2) + cw[1:2] * _shift_rows(u, prev8, 1)
             + cw[2:3] * u)
        carry_s[j] = u[tm - 8:tm]
        tail_ref[0] = u[tm - 8:tm]
        return c

    cg = branch(wg_ref, cwg_ref, cbg_ref, cg_s, tg_ref)
    cl = branch(wl_ref, cwl_ref, cbl_ref, cl_s, tl_ref)
    act = (cg * jax.nn.sigmoid(cg) * cl).astype(BF16)
    contrib = _dot(act, wd_ref[...])

    @pl.when(j == 0)
    def _():
        o_ref[...] = x_ref[...] + contrib

    @pl.when(j > 0)
    def _():
        o_ref[...] += contrib


def _ffn_prompt(x, g, w_up, cw, cb, w_down, *, seq, tm, tf):
    n, d = x.shape
    tm = _row_tile(seq, tm)
    nf = D_FF // tf
    nt = n // tm
    return pl.pallas_call(
        functools.partial(_ffn_prompt_kernel, tm=tm, tiles_per_seq=seq // tm),
        grid=(nt, nf),
        in_specs=[pl.BlockSpec((tm, d), lambda i, j: (i, 0)), _const((1, d)),
                  pl.BlockSpec((d, tf), lambda i, j: (0, j)),
                  pl.BlockSpec((d, tf), lambda i, j: (0, j + nf)),
                  pl.BlockSpec((3, tf), lambda i, j: (0, j)),
                  pl.BlockSpec((3, tf), lambda i, j: (0, j + nf)),
                  pl.BlockSpec((1, tf), lambda i, j: (0, j)),
                  pl.BlockSpec((1, tf), lambda i, j: (0, j + nf)),
                  pl.BlockSpec((tf, d), lambda i, j: (j, 0))],
        out_specs=[pl.BlockSpec((tm, d), lambda i, j: (i, 0)),
                   pl.BlockSpec((1, 8, tf), lambda i, j: (i, 0, j)),
                   pl.BlockSpec((1, 8, tf), lambda i, j: (i, 0, j))],
        out_shape=[jax.ShapeDtypeStruct((n, d), F32),
                   jax.ShapeDtypeStruct((nt, 8, D_FF), F32),
                   jax.ShapeDtypeStruct((nt, 8, D_FF), F32)],
        scratch_shapes=[pltpu.VMEM((tm, d), BF16), pltpu.VMEM((nf, 8, tf), F32),
                        pltpu.VMEM((nf, 8, tf), F32)],
        compiler_params=_params("arbitrary", "arbitrary"),
        name="ffn_prompt",
    )(x, g, w_up, w_up, cw, cw, cb, cb, w_down)


def _ffn_sample_kernel(x_ref, g_ref, wg_ref, wl_ref, cwg_ref, cwl_ref, cbg_ref, cbl_ref, wd_ref,
                       hg_ref, hl_ref, o_ref, tg_ref, tl_ref, hb_s, *, db, steps):
    j = pl.program_id(0)
    n = steps * db

    @pl.when(j == 0)
    def _():
        hb_s[...] = _rms(x_ref[...], g_ref[...]).astype(BF16)

    hb = hb_s[...]

    def branch(w_ref, cw_ref, cb_ref, hist_ref, tail_ref):
        u = _dot(hb, w_ref[...])
        ext = jnp.concatenate([hist_ref[...], u], axis=0)
        cw = cw_ref[...]
        c = cb_ref[...] + cw[0:1] * ext[0:n] + cw[1:2] * ext[db:db + n] + cw[2:3] * ext[2 * db:2 * db + n]
        tail_ref[...] = ext[n:n + 2 * db]
        return c

    cg = branch(wg_ref, cwg_ref, cbg_ref, hg_ref, tg_ref)
    cl = branch(wl_ref, cwl_ref, cbl_ref, hl_ref, tl_ref)
    act = (cg * jax.nn.sigmoid(cg) * cl).astype(BF16)
    contrib = _dot(act, wd_ref[...])

    @pl.when(j == 0)
    def _():
        o_ref[...] = x_ref[...] + contrib

    @pl.when(j > 0)
    def _():
        o_ref[...] += contrib


def _ffn_sample(x, g, w_up, cw, cb, w_down, hist, *, db, steps, tf):
    n, d = x.shape
    nf = D_FF // tf
    return pl.pallas_call(
        functools.partial(_ffn_sample_kernel, db=db, steps=steps),
        grid=(nf,),
        in_specs=[_const((n, d)), _const((1, d)),
                  pl.BlockSpec((d, tf), lambda j: (0, j)),
                  pl.BlockSpec((d, tf), lambda j: (0, j + nf)),
                  pl.BlockSpec((3, tf), lambda j: (0, j)),
                  pl.BlockSpec((3, tf), lambda j: (0, j + nf)),
                  pl.BlockSpec((1, tf), lambda j: (0, j)),
                  pl.BlockSpec((1, tf), lambda j: (0, j + nf)),
                  pl.BlockSpec((tf, d), lambda j: (j, 0)),
                  pl.BlockSpec((2 * db, tf), lambda j: (0, j)),
                  pl.BlockSpec((2 * db, tf), lambda j: (0, j + nf))],
        out_specs=[_const((n, d)),
                   pl.BlockSpec((2 * db, tf), lambda j: (0, j)),
                   pl.BlockSpec((2 * db, tf), lambda j: (0, j))],
        out_shape=[jax.ShapeDtypeStruct((n, d), F32),
                   jax.ShapeDtypeStruct((2 * db, D_FF), F32),
                   jax.ShapeDtypeStruct((2 * db, D_FF), F32)],
        scratch_shapes=[pltpu.VMEM((n, d), BF16)],
        compiler_params=_params("arbitrary"),
        name="ffn_sample",
    )(x, g, w_up, w_up, cw, cw, cb, cb, w_down, hist, hist)


def _mla_proj_kernel(x_ref, g_ref, win_ref, qag_ref, kvg_ref, rkg_ref, wq_ref, gq_ref, gm_ref,
                     wk_ref, gk_ref, wv_ref, cq_ref, sq_ref, ck_ref, sk_ref,
                     qh_ref, kh_ref, v_ref, rows_ref, *, tm):
    hb = _rms(x_ref[...], g_ref[...]).astype(BF16)
    z = _dot(hb, win_ref[...])
    cq = _rms(z[:, 0:Q_RANK], qag_ref[...])
    ckv = _rms(z[:, Q_RANK:Q_RANK + KV_RANK], kvg_ref[...])
    zr = z[:, Q_RANK + KV_RANK:Q_RANK + KV_RANK + 128]
    kr = zr * lax.rsqrt(jnp.sum(zr * zr, axis=-1, keepdims=True) * (1.0 / ROPE_C) + EPS) * rkg_ref[...]
    lane = lax.broadcasted_iota(jnp.int32, (tm, 128), 1)
    half = ROPE_C // 2
    rot_k = jnp.where(lane < half, -pltpu.roll(kr, 128 - half, 1),
                      jnp.where(lane < ROPE_C, pltpu.roll(kr, half, 1), 0.0))
    krf = kr * ck_ref[...] + rot_k * sk_ref[...]
    rows_ref[:, 0:KV_RANK] = ckv
    rows_ref[:, KV_RANK:LAT_C] = krf[:, 0:ROPE_C]

    cqb = cq.astype(BF16)
    ckvb = ckv.astype(BF16)
    gm = gm_ref[...]
    kr_place = pltpu.roll(krf, NOPE_C, 1)
    kr2 = jnp.concatenate([kr_place, kr_place], axis=1)
    cos2 = jnp.concatenate([cq_ref[...], cq_ref[...]], axis=1)
    sin2 = jnp.concatenate([sq_ref[...], sq_ref[...]], axis=1)
    lane2 = lax.broadcasted_iota(jnp.int32, (tm, 256), 1) % 128
    first_half = jnp.logical_and(lane2 >= NOPE_C, lane2 < NOPE_C + half)
    second_half = jnp.logical_and(lane2 >= NOPE_C + half, lane2 < NOPE_C + ROPE_C)
    for c in range(H_C * HEAD_PAD // 256):
        cols = slice(c * 256, (c + 1) * 256)
        zq = _dot(cqb, wq_ref[:, cols])
        qn = zq * lax.rsqrt(_dot((zq * zq).astype(BF16), gm) + EPS) * gq_ref[:, cols]
        rot_q = jnp.where(first_half, -pltpu.roll(qn, 256 - half, 1),
                          jnp.where(second_half, pltpu.roll(qn, half, 1), 0.0))
        qh_ref[:, cols] = (qn * cos2 + rot_q * sin2).astype(BF16)
        zk = _dot(ckvb, wk_ref[:, cols])
        kn = zk * lax.rsqrt(_dot((zk * zk).astype(BF16), gm) + EPS) * gk_ref[:, cols] + kr2
        kh_ref[:, cols] = kn.astype(BF16)
    v_ref[...] = _dot(ckvb, wv_ref[...]).astype(BF16)


def _mla_proj(x, g, win, qag, kvg, rkg, wq, gq, gm, wk, gk, wv, tabs, *, tab_tiles, tm):
    n, d = x.shape
    tm = _row_tile(n, tm)
    hw = H_C * HEAD_PAD
    row = lambda c: pl.BlockSpec((tm, c), lambda i: (i, 0))
    tab = pl.BlockSpec((tm, 128), lambda i: (i % tab_tiles, 0))
    return pl.pallas_call(
        functools.partial(_mla_proj_kernel, tm=tm),
        grid=(n // tm,),
        in_specs=[row(d), _const((1, d)), _const((d, 512)), _const((1, Q_RANK)), _const((1, KV_RANK)),
                  _const((1, 128)), _const((Q_RANK, hw)), _const((1, hw)), _const((256, 256)),
                  _const((KV_RANK, hw)), _const((1, hw)), _const((KV_RANK, H_C * V_C)),
                  tab, tab, tab, tab],
        out_specs=[row(hw), row(hw), row(H_C * V_C), row(LAT_C)],
        out_shape=[jax.ShapeDtypeStruct((n, hw), BF16), jax.ShapeDtypeStruct((n, hw), BF16),
                   jax.ShapeDtypeStruct((n, H_C * V_C), BF16), jax.ShapeDtypeStruct((n, LAT_C), F32)],
        compiler_params=_params("parallel"),
        name="mla_proj",
    )(x, g, win, qag, kvg, rkg, wq, gq, gm, wk, gk, wv, *tabs)


def _diff_sample_kernel(pt_ref, qbd_ref, knew_ref, lam_ref, sg_ref, *refs, n_pages_step, n_steps, lam_init):
    del pt_ref
    page_refs = refs[:n_pages_step]
    o_ref, m_s, l_s, acc_s = refs[n_pages_step:]
    c = pl.program_id(1)

    @pl.when(c == 0)
    def _():
        m_s[...] = jnp.full_like(m_s, NEG_BIG)
        l_s[...] = jnp.zeros_like(l_s)
        acc_s[...] = jnp.zeros_like(acc_s)

    def gather(pr, first_row):
        return jnp.concatenate([pr[pl.ds(first_row + h, PAGE, stride=8), :] for h in range(H_A)],
                               axis=1).astype(BF16)

    qf = qbd_ref[0]
    qb = qf.astype(BF16)
    s = jnp.concatenate([_nt_dot(qb, gather(pr, 0)) for pr in page_refs], axis=1)
    m_old = m_s[...]
    m_new = jnp.maximum(m_old, jnp.max(s, axis=-1, keepdims=True))
    alpha = jnp.exp(m_old - m_new)
    p = jnp.exp(s - m_new)
    l_s[...] = alpha * l_s[...] + jnp.sum(p, axis=-1, keepdims=True)
    pb = p.astype(BF16)
    pv = _dot(pb[:, 0:PAGE], gather(page_refs[0], H_A))
    for i in range(1, n_pages_step):
        pv = pv + _dot(pb[:, i * PAGE:(i + 1) * PAGE], gather(page_refs[i], H_A))
    acc_s[...] = alpha * acc_s[...] + pv
    m_s[...] = m_new

    @pl.when(c == n_steps - 1)
    def _():
        row = lax.broadcasted_iota(jnp.int32, (64, 1), 0)
        qidx = row % 8
        m0 = m_s[...]
        sj = []
        for j in range(4):
            kj = knew_ref[0, j:j + 1, 0:512]
            sj.append(jnp.where(j <= qidx, jnp.sum(qf * kj, axis=-1, keepdims=True), NEG_BIG))
        m1 = jnp.maximum(jnp.maximum(jnp.maximum(m0, sj[0]), jnp.maximum(sj[1], sj[2])), sj[3])
        al = jnp.exp(m0 - m1)
        l1 = al * l_s[...]
        acc = al * acc_s[...]
        for j in range(4):
            pj = jnp.exp(sj[j] - m1)
            l1 = l1 + pj
            acc = acc + pj * knew_ref[0, j:j + 1, 512:1024]
        accn = acc * (1.0 / l1)
        lv = lam_ref[...]
        lam = (jnp.exp(jnp.sum(lv[0:1] * lv[1:2], axis=-1, keepdims=True))
               - jnp.exp(jnp.sum(lv[2:3] * lv[3:4], axis=-1, keepdims=True)) + lam_init)
        o = accn[0:32] - lam * accn[32:64]
        lane_head = lax.broadcasted_iota(jnp.int32, (32, 512), 1) // 128
        row_head = lax.broadcasted_iota(jnp.int32, (32, 512), 0) // 8
        o = jnp.where(lane_head == row_head, o, 0.0)
        o = (o * lax.rsqrt(jnp.sum(o * o, axis=-1, keepdims=True) * (1.0 / 128.0) + EPS)
             * sg_ref[...] * (1.0 - lam_init))
        o_ref[0] = o[0:8] + o[8:16] + o[16:24] + o[24:32]


def _diff_sample(page_table, qbd, knew, lam, sg, cache, *, n_pages_step, lam_init):
    db, n_pages = page_table.shape
    n_steps = n_pages // n_pages_step
    page_specs = [pl.BlockSpec((PAGE * 8, 128), lambda s, c, pt, i=i: (pt[s, c * n_pages_step + i], 0))
                  for i in range(n_pages_step)]
    grid_spec = pltpu.PrefetchScalarGridSpec(
        num_scalar_prefetch=1,
        grid=(db, n_steps),
        in_specs=[pl.BlockSpec((1, 64, 512), lambda s, c, pt: (s, 0, 0)),
                  pl.BlockSpec((1, 8, 1024), lambda s, c, pt: (s, 0, 0)),
                  pl.BlockSpec((4, HD_A), lambda s, c, pt: (0, 0)),
                  pl.BlockSpec((1, 512), lambda s, c, pt: (0, 0))] + page_specs,
        out_specs=pl.BlockSpec((1, 8, 512), lambda s, c, pt: (s, 0, 0)),
        scratch_shapes=[pltpu.VMEM((64, 1), F32), pltpu.VMEM((64, 1), F32), pltpu.VMEM((64, 512), F32)],
    )
    return pl.pallas_call(
        functools.partial(_diff_sample_kernel, n_pages_step=n_pages_step, n_steps=n_steps, lam_init=lam_init),
        grid_spec=grid_spec,
        out_shape=jax.ShapeDtypeStruct((db, 8, 512), F32),
        compiler_params=_params("arbitrary", "arbitrary"),
        name="diff_sample",
    )(page_table, qbd, knew, lam, sg, *([cache] * n_pages_step))


def _mla_sample_kernel(pt_ref, qbd_ref, qrope_ref, qrows_ref, knew_ref, cnew_ref, wkt_ref, wkg_ref, wv_ref,
                       *refs, n_pages_step, n_steps):
    del pt_ref
    page_refs = refs[:n_pages_step]
    o_ref, m_s, l_s, lat_s, qabs_s = refs[n_pages_step:]
    c = pl.program_id(1)
    nk = n_pages_step * PAGE

    @pl.when(c == 0)
    def _():
        m_s[...] = jnp.full_like(m_s, NEG_BIG)
        l_s[...] = jnp.zeros_like(l_s)
        lat_s[...] = jnp.zeros_like(lat_s)
        qabs_s[...] = _dot(qbd_ref[0], wkg_ref[...]).astype(BF16)

    ct = jnp.concatenate([pr[0:KV_RANK, :] for pr in page_refs], axis=1).astype(BF16)
    krt = jnp.concatenate([pr[KV_RANK:LAT_C, :] for pr in page_refs], axis=1)
    krt = jnp.concatenate([krt, jnp.zeros((128 - ROPE_C, nk), F32)], axis=0).astype(BF16)
    kpt = _dot(wkt_ref[...], ct)
    msq = jnp.sum((kpt * kpt).reshape(H_C, NOPE_C, nk), axis=1) * (1.0 / NOPE_C)
    rinv = lax.rsqrt(msq + EPS)
    s = (_dot(qabs_s[...], ct) * jnp.concatenate([rinv] * 4, axis=0)
         + _dot(qrope_ref[0], krt))
    m_old = m_s[...]
    m_new = jnp.maximum(m_old, jnp.max(s, axis=-1, keepdims=True))
    alpha = jnp.exp(m_old - m_new)
    p = jnp.exp(s - m_new)
    l_s[...] = alpha * l_s[...] + jnp.sum(p, axis=-1, keepdims=True)
    lat_s[...] = alpha * lat_s[...] + _nt_dot(p.astype(BF16), ct)
    m_s[...] = m_new

    @pl.when(c == n_steps - 1)
    def _():
        qrows = qrows_ref[0]
        qidx = lax.broadcasted_iota(jnp.int32, (64, 1), 0) // H_C
        m0 = m_s[...]
        sj = []
        for j in range(4):
            kt = jnp.concatenate([knew_ref[0, j]] * 4, axis=0)
            sj.append(jnp.where(j <= qidx, jnp.sum(qrows * kt, axis=-1, keepdims=True), NEG_BIG))
        m1 = jnp.maximum(jnp.maximum(jnp.maximum(m0, sj[0]), jnp.maximum(sj[1], sj[2])), sj[3])
        al = jnp.exp(m0 - m1)
        l1 = al * l_s[...]
        lat = al * lat_s[...]
        for j in range(4):
            pj = jnp.exp(sj[j] - m1)
            l1 = l1 + pj
            lat = lat + pj * cnew_ref[0, j:j + 1, :]
        latn = (lat * (1.0 / l1)).astype(BF16)
        full = _dot(latn, wv_ref[...])
        lane_head = lax.broadcasted_iota(jnp.int32, (64, H_C * V_C), 1) // V_C
        row_head = lax.broadcasted_iota(jnp.int32, (64, H_C * V_C), 0) % H_C
        full = jnp.where(lane_head == row_head, full, 0.0)
        for q in range(4):
            o_ref[0, q:q + 1, :] = jnp.sum(full[q * H_C:(q + 1) * H_C], axis=0, keepdims=True)
        o_ref[0, 4:8, :] = jnp.zeros((4, H_C * V_C), F32)


def _mla_sample(page_table, qbd, qrope, qrows, knew, cnew, wkt, wkg, wv, cache, *, n_pages_step):
    db, n_pages = page_table.shape
    n_steps = n_pages // n_pages_step
    seq3 = lambda a, b: pl.BlockSpec((1, a, b), lambda s, c, pt: (s, 0, 0))
    cst = lambda a, b: pl.BlockSpec((a, b), lambda s, c, pt: (0, 0))
    page_specs = [pl.BlockSpec((LAT_C, PAGE), lambda s, c, pt, i=i: (pt[s, c * n_pages_step + i], 0))
                  for i in range(n_pages_step)]
    grid_spec = pltpu.PrefetchScalarGridSpec(
        num_scalar_prefetch=1,
        grid=(db, n_steps),
        in_specs=[seq3(64, H_C * NOPE_C), seq3(64, 128), seq3(64, 128),
                  pl.BlockSpec((1, 4, H_C, 128), lambda s, c, pt: (s, 0, 0, 0)),
                  seq3(8, KV_RANK), cst(H_C * NOPE_C, KV_RANK), cst(H_C * NOPE_C, KV_RANK),
                  cst(KV_RANK, H_C * V_C)] + page_specs,
        out_specs=seq3(8, H_C * V_C),
        scratch_shapes=[pltpu.VMEM((64, 1), F32), pltpu.VMEM((64, 1), F32), pltpu.VMEM((64, KV_RANK), F32),
                        pltpu.VMEM((64, KV_RANK), BF16)],
    )
    return pl.pallas_call(
        functools.partial(_mla_sample_kernel, n_pages_step=n_pages_step, n_steps=n_steps),
        grid_spec=grid_spec,
        out_shape=jax.ShapeDtypeStruct((db, 8, H_C * V_C), F32),
        compiler_params=_params("arbitrary", "arbitrary"),
        name="mla_sample",
    )(page_table, qbd, qrope, qrows, knew, cnew, wkt, wkg, wv, *([cache] * n_pages_step))


def _group_mean_matrix(sizes, total):
    m = jnp.zeros((total, total), F32)
    off = 0
    for sz, live in sizes:
        if live:
            m = m.at[off:off + sz, off:off + sz].set(1.0 / sz)
        off += sz
    return m.astype(BF16)


def _rope_tables(pos, lane_offset):
    half = ROPE_C // 2
    inv = ROPE_THETA ** (-jnp.arange(half, dtype=F32) / half)
    ang = pos.astype(F32)[:, None] * inv[None, :]
    c2 = jnp.concatenate([jnp.cos(ang), jnp.cos(ang)], axis=1)
    s2 = jnp.concatenate([jnp.sin(ang), jnp.sin(ang)], axis=1)
    cos = jnp.ones((pos.shape[0], 128), F32).at[:, lane_offset:lane_offset + ROPE_C].set(c2)
    sin = jnp.zeros((pos.shape[0], 128), F32).at[:, lane_offset:lane_offset + ROPE_C].set(s2)
    return cos, sin


def kernel(x_prompt, x_sample, cache_kv_diff, cache_mla, state_lru_h, state_lru_conv, state_ffn_conv, page_table,
           norm_mix, norm_ffn, w_in_e, qk_norm_a, lambda_a, subln_a, conv_w_lru, conv_b_lru, gate_w_lru,
           gate_b_lru, lambda_lru, w_out_e, w_in_c, q_a_norm, kv_a_norm, w_uq, w_ukv, qn_c, rn_c, w_out_c,
           w_up, conv_w_ffn, conv_b_ffn, w_down):
    B, T, D = x_prompt.shape
    DB, S, _ = x_sample.shape
    n_pages = page_table.shape[1]
    n_past = n_pages * PAGE
    n_pool = cache_kv_diff.shape[1]
    depth = norm_mix.shape[0]
    NP, NS = B * T, S * DB
    pt = page_table.astype(jnp.int32)

    xp = x_prompt.reshape(NP, D)
    xs = jnp.swapaxes(x_sample, 0, 1).reshape(NS, D)

    def to_seq_major(a):
        return jnp.swapaxes(a.reshape((S, DB) + a.shape[1:]), 0, 1)

    pos_p = jnp.arange(T, dtype=jnp.int32)
    pos_s = jnp.repeat(n_past + jnp.arange(S, dtype=jnp.int32), DB)
    tabs_p = _rope_tables(pos_p, NOPE_C) + _rope_tables(pos_p, 0)
    tabs_s = _rope_tables(pos_s, NOPE_C) + _rope_tables(pos_s, 0)

    kv_p, kv_s, mla_p, mla_s = [], [], [], []
    lh_p, lh_s, lc_p, lc_s, fc_p, fc_s = [], [], [], [], [], []

    for l in range(depth):
        g_mix = norm_mix[l].reshape(1, D)
        if l % 2 == 0:
            e = l // 2
            lam_init = 0.8 - 0.6 * math.exp(-0.3 * l)
            w_in = w_in_e[e].astype(BF16)
            qs = jnp.tile(qk_norm_a[e, 0].reshape(1, 2 * HD_A), (1, H_A)) * (1.0 / math.sqrt(HD_A))
            ks = jnp.tile(qk_norm_a[e, 1].reshape(1, 2 * HD_A), (1, H_A))
            gm64 = _group_mean_matrix([(HD_A, True)] * (2 * H_A), 2 * H_A * HD_A)
            lam_v = lambda_a[e]
            sg = subln_a[e].reshape(1, 2 * HD_A)
            cw, cb = conv_w_lru[e], conv_b_lru[e].reshape(1, D_RNN)
            gw = jnp.zeros((2, 8, 64, 8, 64), F32)
            gw = gw.at[:, jnp.arange(8), :, jnp.arange(8), :].set(jnp.swapaxes(gate_w_lru[e], 0, 1))
            gw = jnp.concatenate([gw[0].reshape(D_RNN, D_RNN), gw[1].reshape(D_RNN, D_RNN)], axis=1).astype(BF16)
            gb = gate_b_lru[e].reshape(1, 2 * D_RNN)
            lam_lru = lambda_lru[e].reshape(1, D_RNN)
            w_out = w_out_e[e].astype(BF16)

            qb_p, kb_p, vb_p, kvf_p, xg_p = _even_proj(xp, g_mix, w_in, qs, ks, gm64, 512)
            qb_s, kb_s, vb_s, kvf_s, xg_s = _even_proj(xs, g_mix, w_in, qs, ks, gm64, 512)

            a_p = _flash(qb_p, kb_p, vb_p, (lam_v, sg), diff=True, batch=B, seq=T, tb=512, lam_init=lam_init)

            q_sm = to_seq_major(qb_s.astype(F32)).reshape(DB, S, H_A, 2, HD_A)
            eye_h = jnp.eye(H_A, dtype=F32)
            eye_m = jnp.eye(2, dtype=F32)
            qbd = jnp.einsum('bqhmd,hg,mn->bmhqgnd', q_sm, eye_h, eye_m)
            qbd = jnp.pad(qbd, ((0, 0), (0, 0), (0, 0), (0, 8 - S), (0, 0), (0, 0), (0, 0)))
            qbd = qbd.reshape(DB, 2 * H_A * 8, H_A * 2 * HD_A)
            knew = jnp.pad(to_seq_major(kvf_s), ((0, 0), (0, 8 - S), (0, 0)))
            cache_d = cache_kv_diff.reshape(cache_kv_diff.shape[0] * n_pool * PAGE * 2 * H_A, 2 * HD_A)
            sg4 = jnp.tile(sg, (1, H_A))
            a_s = _diff_sample(pt + e * n_pool, qbd, knew, lam_v, sg4, cache_d,
                               n_pages_step=min(16, n_pages), lam_init=lam_init)
            a_s = jnp.swapaxes(a_s[:, :S], 0, 1).reshape(NS, H_A * 2 * HD_A).astype(BF16)

            y_p, hl_pp, ct_pp = _lru_prompt(xg_p, cw, cb, gw, gb, lam_lru, batch=B, seq=T, tm=256)
            hist = jnp.swapaxes(state_lru_conv[e], 0, 1).reshape(3 * DB, D_RNN)
            y_s, hl_ss, ct_ss = _lru_sample(xg_s, hist, state_lru_h[e], cw, cb, gw, gb, lam_lru, db=DB, steps=S)

            xp = _mm_res(xp, [a_p, y_p], w_out, 512)
            xs = _mm_res(xs, [a_s, y_s], w_out, 512)

            kv_p.append(kvf_p.reshape(B, T, 2, H_A, 2 * HD_A))
            kv_s.append(to_seq_major(kvf_s).reshape(DB, S, 2, H_A, 2 * HD_A))
            nt = hl_pp.shape[0] // B
            lh_p.append(hl_pp.reshape(B, nt, 8, D_RNN)[:, -1, -1])
            lc_p.append(ct_pp.reshape(B, nt, 8, D_RNN)[:, -1, 5:8])
            lh_s.append(hl_ss)
            lc_s.append(jnp.swapaxes(ct_ss.reshape(3, DB, D_RNN), 0, 1))
        else:
            o = l // 2
            scale = 1.0 / math.sqrt(NOPE_C + ROPE_C)
            pad_h = HEAD_PAD - NOPE_C - ROPE_C
            win = jnp.pad(w_in_c[o], ((0, 0), (0, 512 - w_in_c.shape[2]))).astype(BF16)
            qag = q_a_norm[o].reshape(1, Q_RANK)
            kvg = kv_a_norm[o].reshape(1, KV_RANK)
            rkg = jnp.pad(rn_c[o, 1], (0, 128 - ROPE_C)).reshape(1, 128)
            wq = jnp.pad(w_uq[o], ((0, 0), (0, 0), (0, pad_h))).reshape(Q_RANK, H_C * HEAD_PAD).astype(BF16)
            gq = jnp.tile(jnp.concatenate([qn_c[o, 0], rn_c[o, 0], jnp.zeros((pad_h,), F32)]) * scale,
                          H_C).reshape(1, H_C * HEAD_PAD)
            gm256 = _group_mean_matrix([(NOPE_C, True), (ROPE_C, True), (pad_h, False)] * 2, 256)
            wk_nope = w_ukv[o][:, :, :NOPE_C]
            wk = jnp.pad(wk_nope, ((0, 0), (0, 0), (0, HEAD_PAD - NOPE_C))).reshape(KV_RANK, H_C * HEAD_PAD)
            wk = wk.astype(BF16)
            gk = jnp.tile(jnp.concatenate([qn_c[o, 1], jnp.zeros((HEAD_PAD - NOPE_C,), F32)]),
                          H_C).reshape(1, H_C * HEAD_PAD)
            wv = w_ukv[o][:, :, NOPE_C:].reshape(KV_RANK, H_C * V_C).astype(BF16)
            wkt = jnp.transpose(wk_nope.reshape(KV_RANK, H_C * NOPE_C))
            wkg = (wkt * jnp.tile(qn_c[o, 1], H_C)[:, None]).astype(BF16)
            wkt = wkt.astype(BF16)
            w_out = w_out_c[o].astype(BF16)

            tm_p = _row_tile(T, 512)
            qh_p, kh_p, v_p, rows_p = _mla_proj(xp, g_mix, win, qag, kvg, rkg, wq, gq, gm256, wk, gk, wv, tabs_p,
                                                tab_tiles=T // tm_p, tm=tm_p)
            qh_s, kh_s, v_s, rows_s = _mla_proj(xs, g_mix, win, qag, kvg, rkg, wq, gq, gm256, wk, gk, wv, tabs_s,
                                                tab_tiles=1, tm=NS)

            o_p = _flash(qh_p, kh_p, v_p, (), diff=False, batch=B, seq=T, tb=512)

            qh_sm = to_seq_major(qh_s.astype(F32)).reshape(DB, S, H_C, HEAD_PAD)
            qrows = qh_sm.reshape(DB, S * H_C, HEAD_PAD)
            qn = qh_sm[..., :NOPE_C]
            qbd = jnp.einsum('bqhd,hg->bqhgd', qn, jnp.eye(H_C, dtype=F32)).reshape(DB, S * H_C, H_C * NOPE_C)
            qrope = jnp.pad(qh_sm[..., NOPE_C:NOPE_C + ROPE_C],
                            ((0, 0), (0, 0), (0, 0), (0, 128 - ROPE_C))).reshape(DB, S * H_C, 128)
            knew = to_seq_major(kh_s.astype(F32)).reshape(DB, S, H_C, HEAD_PAD)
            rows_sm = to_seq_major(rows_s)
            cnew = jnp.pad(rows_sm[..., :KV_RANK], ((0, 0), (0, 8 - S), (0, 0)))
            cache_m = jnp.swapaxes(cache_mla, 2, 3).reshape(cache_mla.shape[0] * n_pool * LAT_C, PAGE)
            o_s = _mla_sample(pt + o * n_pool, qbd.astype(BF16), qrope.astype(BF16), qrows, knew, cnew,
                              wkt, wkg, wv, cache_m, n_pages_step=min(16, n_pages))
            o_s = jnp.swapaxes(o_s[:, :S], 0, 1).reshape(NS, H_C * V_C).astype(BF16)

            xp = _mm_res(xp, [o_p], w_out, 512)
            xs = _mm_res(xs, [o_s], w_out, 512)
            mla_p.append(rows_p.reshape(B, T, LAT_C))
            mla_s.append(rows_sm)

        g_ffn = norm_ffn[l].reshape(1, D)
        wu = w_up[l].astype(BF16)
        wd = w_down[l].astype(BF16)
        cwf, cbf = conv_w_ffn[l], conv_b_ffn[l].reshape(1, 2 * D_FF)
        xp, tg, tl = _ffn_prompt(xp, g_ffn, wu, cwf, cbf, wd, seq=T, tm=512, tf=1408)
        ntf = tg.shape[0] // B
        tail = jnp.concatenate([tg, tl], axis=-1).reshape(B, ntf, 8, 2 * D_FF)
        fc_p.append(tail[:, -1, 6:8])
        hist_f = jnp.swapaxes(state_ffn_conv[l], 0, 1).reshape(2 * DB, 2 * D_FF)
        xs, tgs, tls = _ffn_sample(xs, g_ffn, wu, cwf, cbf, wd, hist_f, db=DB, steps=S, tf=1408)
        fc_s.append(jnp.swapaxes(jnp.concatenate([tgs, tls], axis=-1).reshape(2, DB, 2 * D_FF), 0, 1))

    y_p = xp.reshape(B, T, D)
    y_s = to_seq_major(xs)
    return (y_p, y_s, jnp.stack(kv_p), jnp.stack(kv_s), jnp.stack(mla_p), jnp.stack(mla_s),
            jnp.stack(lh_p), jnp.stack(lh_s), jnp.stack(lc_p), jnp.stack(lc_s), jnp.stack(fc_p), jnp.stack(fc_s))
```

```python
import functools
import math

import jax
import jax.numpy as jnp
from jax import lax
from jax.experimental import pallas as pl
from jax.experimental.pallas import tpu as pltpu

F32 = jnp.float32
BF16 = jnp.bfloat16

EPS = 1e-6
NEG_BIG = -1e30
ROPE_THETA = 10000.0
LRU_C = 8.0
PAGE = 128

H_A = 4
HD_A = 64
D_RNN = 512
H_C = 16
Q_RANK = 256
KV_RANK = 128
NOPE_C = 64
ROPE_C = 32
V_C = 64
LAT_C = KV_RANK + ROPE_C
HEAD_PAD = 128
D_FF = 2816

VMEM_LIMIT_BYTES = 56 * 1024 * 1024


def _params(*semantics):
    return pltpu.CompilerParams(dimension_semantics=semantics,
                                vmem_limit_bytes=VMEM_LIMIT_BYTES)


def _row_tile(n, want):
    t = min(n, want)
    assert n % t == 0
    return t


def _const(shape):
    return pl.BlockSpec(shape, lambda *_: (0,) * len(shape))


def _rms(x, g):
    return x * lax.rsqrt(jnp.mean(x * x, axis=-1, keepdims=True) + EPS) * g


def _nt_dot(a, b):
    return lax.dot_general(a, b, (((1,), (1,)), ((), ())), preferred_element_type=F32)


def _dot(a, b):
    return jnp.dot(a, b, preferred_element_type=F32)


def _gelu_tanh(x):
    return 0.5 * x * (1.0 + jnp.tanh(math.sqrt(2.0 / math.pi) * (x + 0.044715 * x * x * x)))


def _softplus(y):
    return jnp.maximum(y, 0.0) + jnp.log1p(jnp.exp(-jnp.abs(y)))


def _shift_rows(x, prev8, s):
    xr = pltpu.roll(x, s, 0)
    row8 = lax.broadcasted_iota(jnp.int32, prev8.shape, 0)
    top = jnp.where(row8 < s, pltpu.roll(prev8, s, 0), xr[0:8])
    return jnp.concatenate([top, xr[8:]], axis=0)


def _even_proj_kernel(x_ref, g_ref, w_ref, qs_ref, ks_ref, gm_ref,
                      qb_ref, kb_ref, vb_ref, kv_ref, xg_ref):
    hb = _rms(x_ref[...], g_ref[...]).astype(BF16)
    z = _dot(hb, w_ref[...])
    zq, zk, zv = z[:, 0:512], z[:, 512:1024], z[:, 1024:1536]
    gm = gm_ref[...]
    mq = _dot((zq * zq).astype(BF16), gm)
    mk = _dot((zk * zk).astype(BF16), gm)
    q = zq * lax.rsqrt(mq + EPS) * qs_ref[...]
    k = zk * lax.rsqrt(mk + EPS) * ks_ref[...]
    qb_ref[...] = q.astype(BF16)
    kb_ref[...] = k.astype(BF16)
    vb_ref[...] = zv.astype(BF16)
    kv_ref[:, 0:512] = k
    kv_ref[:, 512:1024] = zv
    xg_ref[...] = z[:, 1536:2560]


def _even_proj(x, g, w, qs, ks, gm, tm):
    n = x.shape[0]
    tm = _row_tile(n, tm)
    row = lambda c: pl.BlockSpec((tm, c), lambda i: (i, 0))
    return pl.pallas_call(
        _even_proj_kernel,
        grid=(n // tm,),
        in_specs=[row(1024), _const((1, 1024)), _const((1024, 2560)), _const((1, 512)),
                  _const((1, 512)), _const((512, 512))],
        out_specs=[row(512), row(512), row(512), row(1024), row(1024)],
        out_shape=[jax.ShapeDtypeStruct((n, 512), BF16), jax.ShapeDtypeStruct((n, 512), BF16),
                   jax.ShapeDtypeStruct((n, 512), BF16), jax.ShapeDtypeStruct((n, 1024), F32),
                   jax.ShapeDtypeStruct((n, 1024), F32)],
        compiler_params=_params("parallel"),
        name="even_proj",
    )(x, g, w, qs, ks, gm)


def _online_update(s, m, acc, vext):
    m_new = jnp.maximum(m, jnp.max(s, axis=-1, keepdims=True))
    alpha = jnp.exp(m - m_new)
    p = jnp.exp(s - m_new).astype(BF16)
    return m_new, alpha * acc + _dot(p, vext)


def _flash_kernel(*refs, diff, tb, npair, lam_init):
    if diff:
        q_ref, k_ref, v_ref, lam_ref, sg_ref, o_ref = refs
    else:
        q_ref, k_ref, v_ref, o_ref = refs
    qi = pl.program_id(2)
    qw = 128 if diff else 256
    nchain = 2 * npair
    lane = lax.broadcasted_iota(jnp.int32, (tb, 128), 1)
    qs = []
    for g in range(npair):
        q = q_ref[:, g * qw:(g + 1) * qw]
        if diff:
            zero = jnp.zeros_like(q)
            qs.append(jnp.where(lane < HD_A, q, zero))
            qs.append(jnp.where(lane >= HD_A, q, zero))
        else:
            qs += [q[:, 0:128], q[:, 128:256]]

    ones = jnp.ones((tb, 128), BF16)

    def scores(j):
        start = pl.multiple_of(j * tb, tb)
        out = []
        for g in range(npair):
            kblk = k_ref[pl.ds(start, tb), g * qw:(g + 1) * qw]
            ks = [kblk, kblk] if diff else [kblk[:, 0:128], kblk[:, 128:256]]
            out += [_nt_dot(qs[2 * g + c], ks[c]) for c in range(2)]
        return out

    def values(j):
        start = pl.multiple_of(j * tb, tb)
        return [jnp.concatenate([v_ref[pl.ds(start, tb), g * 128:(g + 1) * 128], ones], axis=1)
                for g in range(npair)]

    def body(j, carry):
        s, vs = scores(j), values(j)
        out = []
        for c in range(nchain):
            out += list(_online_update(s[c], carry[2 * c], carry[2 * c + 1], vs[c // 2]))
        return tuple(out)

    col1 = jnp.full((tb, 1), NEG_BIG, F32)
    zacc = jnp.zeros((tb, 256), F32)
    carry = lax.fori_loop(0, qi, body, (col1, zacc) * nchain)

    s, vs = scores(qi), values(qi)
    r = lax.broadcasted_iota(jnp.int32, (tb, tb), 0)
    c = lax.broadcasted_iota(jnp.int32, (tb, tb), 1)
    keep = c <= r
    for g in range(npair):
        os_ = []
        for c2 in (2 * g, 2 * g + 1):
            _, a = _online_update(jnp.where(keep, s[c2], NEG_BIG), carry[2 * c2], carry[2 * c2 + 1], vs[g])
            os_.append(a[:, 0:128] * (1.0 / a[:, 128:129]))
        if diff:
            lv = lam_ref[...]
            lam = (jnp.exp(jnp.sum(lv[0:1] * lv[1:2], axis=-1, keepdims=True))
                   - jnp.exp(jnp.sum(lv[2:3] * lv[3:4], axis=-1, keepdims=True)) + lam_init)
            o = os_[0] - lam * os_[1]
            o = _rms(o, sg_ref[...]) * (1.0 - lam_init)
        else:
            o = jnp.where(lane < V_C, os_[0], os_[1])
        o_ref[:, g * 128:(g + 1) * 128] = o.astype(o_ref.dtype)


def _flash(q, k, v, extra, *, diff, batch, seq, tb, npair, lam_init=0.0):
    n = q.shape[0]
    tb = _row_tile(seq, tb)
    nq = seq // tb
    qw = (128 if diff else 256) * npair
    groups = q.shape[1] // qw
    in_specs = [pl.BlockSpec((tb, qw), lambda b, g, i: (b * nq + i, g)),
                pl.BlockSpec((seq, qw), lambda b, g, i: (b, g)),
                pl.BlockSpec((seq, 128 * npair), lambda b, g, i: (b, g))]
    in_specs += [_const(e.shape) for e in extra]
    return pl.pallas_call(
        functools.partial(_flash_kernel, diff=diff, tb=tb, npair=npair, lam_init=lam_init),
        grid=(batch, groups, nq),
        in_specs=in_specs,
        out_specs=pl.BlockSpec((tb, 128 * npair), lambda b, g, i: (b * nq + i, g)),
        out_shape=jax.ShapeDtypeStruct((n, groups * 128 * npair), BF16),
        compiler_params=_params("parallel", "parallel", "arbitrary"),
        name="flash_diff" if diff else "flash_mla",
    )(q, k, v, *extra)


def _lru_gates(xc, gw_ref, gb_ref, lam_ref):
    g = _dot(xc.astype(BF16), gw_ref[...]) + gb_ref[...]
    r = jax.nn.sigmoid(g[:, 0:D_RNN])
    i = jax.nn.sigmoid(g[:, D_RNN:2 * D_RNN])
    a = jnp.exp(-LRU_C * r * _softplus(-lam_ref[...]))
    mult = jnp.sqrt(1.0 - a * a)
    return a, mult, i


def _lru_prompt_kernel(x_ref, gr_ref, cw_ref, cb_ref, gw_ref, gb_ref, lam_ref,
                       y_ref, hl_ref, ct_ref, xprev_s, h_s, *, tm):
    ti = pl.program_id(1)

    @pl.when(ti == 0)
    def _():
        xprev_s[...] = jnp.zeros_like(xprev_s)
        h_s[...] = jnp.zeros_like(h_s)

    x = x_ref[...]
    prev8 = xprev_s[...]
    cw = cw_ref[...]
    xc = (cb_ref[...] + cw[0:1] * _shift_rows(x, prev8, 3) + cw[1:2] * _shift_rows(x, prev8, 2)
          + cw[2:3] * _shift_rows(x, prev8, 1) + cw[3:4] * x)
    a, mult, i = _lru_gates(xc, gw_ref, gb_ref, lam_ref)
    row = lax.broadcasted_iota(jnp.int32, (tm, D_RNN), 0)
    mult = jnp.where(jnp.logical_and(row == 0, ti == 0), 1.0, mult)
    b = mult * i * xc
    d = 1
    while d < tm:
        ar = jnp.where(row >= d, pltpu.roll(a, d, 0), 1.0)
        br = jnp.where(row >= d, pltpu.roll(b, d, 0), 0.0)
        b = a * br + b
        a = a * ar
        d *= 2
    h = a * h_s[7:8] + b
    y_ref[...] = (h * _gelu_tanh(gr_ref[...])).astype(BF16)
    h_s[...] = h[tm - 8:tm]
    hl_ref[0] = h[tm - 8:tm]
    ct_ref[0] = x[tm - 8:tm]
    xprev_s[...] = x[tm - 8:tm]


def _lru_prompt(xg, cw, cb, gw, gb, lam, *, batch, seq, tm):
    n = xg.shape[0]
    tm = _row_tile(seq, tm)
    nt = seq // tm
    return pl.pallas_call(
        functools.partial(_lru_prompt_kernel, tm=tm),
        grid=(batch, nt),
        in_specs=[pl.BlockSpec((tm, D_RNN), lambda b, t: (b * nt + t, 0)),
                  pl.BlockSpec((tm, D_RNN), lambda b, t: (b * nt + t, 1)),
                  _const((4, D_RNN)), _const((1, D_RNN)), _const((D_RNN, 2 * D_RNN)),
                  _const((1, 2 * D_RNN)), _const((1, D_RNN))],
        out_specs=[pl.BlockSpec((tm, D_RNN), lambda b, t: (b * nt + t, 0)),
                   pl.BlockSpec((1, 8, D_RNN), lambda b, t: (b * nt + t, 0, 0)),
                   pl.BlockSpec((1, 8, D_RNN), lambda b, t: (b * nt + t, 0, 0))],
        out_shape=[jax.ShapeDtypeStruct((n, D_RNN), BF16),
                   jax.ShapeDtypeStruct((batch * nt, 8, D_RNN), F32),
                   jax.ShapeDtypeStruct((batch * nt, 8, D_RNN), F32)],
        scratch_shapes=[pltpu.VMEM((8, D_RNN), F32), pltpu.VMEM((8, D_RNN), F32)],
        compiler_params=_params("arbitrary", "arbitrary"),
        name="lru_prompt",
    )(xg, xg, cw, cb, gw, gb, lam)


def _lru_sample_kernel(x_ref, gr_ref, hist_ref, h0_ref, cw_ref, cb_ref, gw_ref, gb_ref, lam_ref,
                       y_ref, hl_ref, ct_ref, *, db, steps):
    n = steps * db
    ext = jnp.concatenate([hist_ref[...], x_ref[...]], axis=0)
    cw = cw_ref[...]
    xc = (cb_ref[...] + cw[0:1] * ext[0:n] + cw[1:2] * ext[db:db + n]
          + cw[2:3] * ext[2 * db:2 * db + n] + cw[3:4] * ext[3 * db:3 * db + n])
    a, mult, i = _lru_gates(xc, gw_ref, gb_ref, lam_ref)
    b = mult * i * xc
    h = h0_ref[...]
    hs = []
    for t in range(steps):
        h = a[t * db:(t + 1) * db] * h + b[t * db:(t + 1) * db]
        hs.append(h)
    hseq = jnp.concatenate(hs, axis=0)
    y_ref[...] = (hseq * _gelu_tanh(gr_ref[...])).astype(BF16)
    hl_ref[...] = h
    ct_ref[...] = ext[n:n + 3 * db]


def _lru_sample(xg, hist, h0, cw, cb, gw, gb, lam, *, db, steps):
    n = steps * db
    return pl.pallas_call(
        functools.partial(_lru_sample_kernel, db=db, steps=steps),
        grid=(1,),
        in_specs=[pl.BlockSpec((n, D_RNN), lambda i: (0, 0)),
                  pl.BlockSpec((n, D_RNN), lambda i: (0, 1)),
                  _const((3 * db, D_RNN)), _const((db, D_RNN)),
                  _const((4, D_RNN)), _const((1, D_RNN)), _const((D_RNN, 2 * D_RNN)),
                  _const((1, 2 * D_RNN)), _const((1, D_RNN))],
        out_specs=[_const((n, D_RNN)), _const((db, D_RNN)), _const((3 * db, D_RNN))],
        out_shape=[jax.ShapeDtypeStruct((n, D_RNN), BF16),
                   jax.ShapeDtypeStruct((db, D_RNN), F32),
                   jax.ShapeDtypeStruct((3 * db, D_RNN), F32)],
        compiler_params=_params("arbitrary"),
        name="lru_sample",
    )(xg, xg, hist, h0, cw, cb, gw, gb, lam)


def _mm_res_kernel(*refs, n_in):
    res_ref, o_ref = refs[0], refs[-1]
    acc = res_ref[...]
    for a_ref, w_ref in zip(refs[1:1 + n_in], refs[1 + n_in:1 + 2 * n_in]):
        acc = acc + _dot(a_ref[...], w_ref[...])
    o_ref[...] = acc


def _mm_res(res, acts, w, tm):
    n, d = res.shape
    tm = _row_tile(n, tm)
    n_in = len(acts)
    kw = w.shape[0] // n_in
    in_specs = [pl.BlockSpec((tm, d), lambda i: (i, 0))]
    in_specs += [pl.BlockSpec((tm, kw), lambda i: (i, 0)) for _ in acts]
    in_specs += [pl.BlockSpec((kw, d), lambda i, j=j: (j, 0)) for j in range(n_in)]
    return pl.pallas_call(
        functools.partial(_mm_res_kernel, n_in=n_in),
        grid=(n // tm,),
        in_specs=in_specs,
        out_specs=pl.BlockSpec((tm, d), lambda i: (i, 0)),
        out_shape=jax.ShapeDtypeStruct((n, d), F32),
        compiler_params=_params("parallel"),
        name="out_proj",
    )(res, *acts, *([w] * n_in))


def _ffn_prompt_kernel(x_ref, g_ref, wg_ref, wl_ref, cwg_ref, cwl_ref, cbg_ref, cbl_ref, wd_ref,
                       o_ref, tg_ref, tl_ref, hb_s, cg_s, cl_s, *, tm, tiles_per_seq):
    i = pl.program_id(0)
    j = pl.program_id(1)

    @pl.when(j == 0)
    def _():
        hb_s[...] = _rms(x_ref[...], g_ref[...]).astype(BF16)

    hb = hb_s[...]
    first = (i % tiles_per_seq) == 0

    def branch(w_ref, cw_ref, cb_ref, carry_s, tail_ref):
        u = _dot(hb, w_ref[...])

        @pl.when(first)
        def _():
            carry_s[j] = jnp.zeros((8, u.shape[1]), F32)

        prev8 = carry_s[j]
        cw = cw_ref[...]
        c = (cb_ref[...] + cw[0:1] * _shift_rows(u, prev8, 2) + cw[1:2] * _shift_rows(u, prev8, 1)
             + cw[2:3] * u)
        carry_s[j] = u[tm - 8:tm]
        tail_ref[0] = u[tm - 8:tm]
        return c

    cg = branch(wg_ref, cwg_ref, cbg_ref, cg_s, tg_ref)
    cl = branch(wl_ref, cwl_ref, cbl_ref, cl_s, tl_ref)
    act = (cg * jax.nn.sigmoid(cg) * cl).astype(BF16)
    contrib = _dot(act, wd_ref[...])

    @pl.when(j == 0)
    def _():
        o_ref[...] = x_ref[...] + contrib

    @pl.when(j > 0)
    def _():
        o_ref[...] += contrib


def _ffn_prompt(x, g, w_up, cw, cb, w_down, *, seq, tm, tf):
    n, d = x.shape
    tm = _row_tile(seq, tm)
    nf = D_FF // tf
    nt = n // tm
    return pl.pallas_call(
        functools.partial(_ffn_prompt_kernel, tm=tm, tiles_per_seq=seq // tm),
        grid=(nt, nf),
        in_specs=[pl.BlockSpec((tm, d), lambda i, j: (i, 0)), _const((1, d)),
                  pl.BlockSpec((d, tf), lambda i, j: (0, j)),
                  pl.BlockSpec((d, tf), lambda i, j: (0, j + nf)),
                  pl.BlockSpec((3, tf), lambda i, j: (0, j)),
                  pl.BlockSpec((3, tf), lambda i, j: (0, j + nf)),
                  pl.BlockSpec((1, tf), lambda i, j: (0, j)),
                  pl.BlockSpec((1, tf), lambda i, j: (0, j + nf)),
                  pl.BlockSpec((tf, d), lambda i, j: (j, 0))],
        out_specs=[pl.BlockSpec((tm, d), lambda i, j: (i, 0)),
                   pl.BlockSpec((1, 8, tf), lambda i, j: (i, 0, j)),
                   pl.BlockSpec((1, 8, tf), lambda i, j: (i, 0, j))],
        out_shape=[jax.ShapeDtypeStruct((n, d), F32),
                   jax.ShapeDtypeStruct((nt, 8, D_FF), F32),
                   jax.ShapeDtypeStruct((nt, 8, D_FF), F32)],
        scratch_shapes=[pltpu.VMEM((tm, d), BF16), pltpu.VMEM((nf, 8, tf), F32),
                        pltpu.VMEM((nf, 8, tf), F32)],
        compiler_params=_params("arbitrary", "arbitrary"),
        name="ffn_prompt",
    )(x, g, w_up, w_up, cw, cw, cb, cb, w_down)


def _ffn_sample_kernel(x_ref, g_ref, wg_ref, wl_ref, cwg_ref, cwl_ref, cbg_ref, cbl_ref, wd_ref,
                       hg_ref, hl_ref, o_ref, tg_ref, tl_ref, hb_s, *, db, steps):
    j = pl.program_id(0)
    n = steps * db

    @pl.when(j == 0)
    def _():
        hb_s[...] = _rms(x_ref[...], g_ref[...]).astype(BF16)

    hb = hb_s[...]

    def branch(w_ref, cw_ref, cb_ref, hist_ref, tail_ref):
        u = _dot(hb, w_ref[...])
        ext = jnp.concatenate([hist_ref[...], u], axis=0)
        cw = cw_ref[...]
        c = cb_ref[...] + cw[0:1] * ext[0:n] + cw[1:2] * ext[db:db + n] + cw[2:3] * ext[2 * db:2 * db + n]
        tail_ref[...] = ext[n:n + 2 * db]
        return c

    cg = branch(wg_ref, cwg_ref, cbg_ref, hg_ref, tg_ref)
    cl = branch(wl_ref, cwl_ref, cbl_ref, hl_ref, tl_ref)
    act = (cg * jax.nn.sigmoid(cg) * cl).astype(BF16)
    contrib = _dot(act, wd_ref[...])

    @pl.when(j == 0)
    def _():
        o_ref[...] = x_ref[...] + contrib

    @pl.when(j > 0)
    def _():
        o_ref[...] += contrib


def _ffn_sample(x, g, w_up, cw, cb, w_down, hist, *, db, steps, tf):
    n, d = x.shape
    nf = D_FF // tf
    return pl.pallas_call(
        functools.partial(_ffn_sample_kernel, db=db, steps=steps),
        grid=(nf,),
        in_specs=[_const((n, d)), _const((1, d)),
                  pl.BlockSpec((d, tf), lambda j: (0, j)),
                  pl.BlockSpec((d, tf), lambda j: (0, j + nf)),
                  pl.BlockSpec((3, tf), lambda j: (0, j)),
                  pl.BlockSpec((3, tf), lambda j: (0, j + nf)),
                  pl.BlockSpec((1, tf), lambda j: (0, j)),
                  pl.BlockSpec((1, tf), lambda j: (0, j + nf)),
                  pl.BlockSpec((tf, d), lambda j: (j, 0)),
                  pl.BlockSpec((2 * db, tf), lambda j: (0, j)),
                  pl.BlockSpec((2 * db, tf), lambda j: (0, j + nf))],
        out_specs=[_const((n, d)),
                   pl.BlockSpec((2 * db, tf), lambda j: (0, j)),
                   pl.BlockSpec((2 * db, tf), lambda j: (0, j))],
        out_shape=[jax.ShapeDtypeStruct((n, d), F32),
                   jax.ShapeDtypeStruct((2 * db, D_FF), F32),
                   jax.ShapeDtypeStruct((2 * db, D_FF), F32)],
        scratch_shapes=[pltpu.VMEM((n, d), BF16)],
        compiler_params=_params("arbitrary"),
        name="ffn_sample",
    )(x, g, w_up, w_up, cw, cw, cb, cb, w_down, hist, hist)


def _mla_proj_kernel(x_ref, g_ref, win_ref, qag_ref, kvg_ref, rkg_ref, wq_ref, gq_ref, gm_ref,
                     wk_ref, gk_ref, wv_ref, cq_ref, sq_ref, ck_ref, sk_ref,
                     qh_ref, kh_ref, v_ref, rows_ref, *, tm):
    hb = _rms(x_ref[...], g_ref[...]).astype(BF16)
    z = _dot(hb, win_ref[...])
    cq = _rms(z[:, 0:Q_RANK], qag_ref[...])
    ckv = _rms(z[:, Q_RANK:Q_RANK + KV_RANK], kvg_ref[...])
    zr = z[:, Q_RANK + KV_RANK:Q_RANK + KV_RANK + 128]
    kr = zr * lax.rsqrt(jnp.sum(zr * zr, axis=-1, keepdims=True) * (1.0 / ROPE_C) + EPS) * rkg_ref[...]
    lane = lax.broadcasted_iota(jnp.int32, (tm, 128), 1)
    half = ROPE_C // 2
    rot_k = jnp.where(lane < half, -pltpu.roll(kr, 128 - half, 1),
                      jnp.where(lane < ROPE_C, pltpu.roll(kr, half, 1), 0.0))
    krf = kr * ck_ref[...] + rot_k * sk_ref[...]
    rows_ref[:, 0:KV_RANK] = ckv
    rows_ref[:, KV_RANK:LAT_C] = krf[:, 0:ROPE_C]

    cqb = cq.astype(BF16)
    ckvb = ckv.astype(BF16)
    gm = gm_ref[...]
    kr_place = pltpu.roll(krf, NOPE_C, 1)
    kr2 = jnp.concatenate([kr_place, kr_place], axis=1)
    cos2 = jnp.concatenate([cq_ref[...], cq_ref[...]], axis=1)
    sin2 = jnp.concatenate([sq_ref[...], sq_ref[...]], axis=1)
    lane2 = lax.broadcasted_iota(jnp.int32, (tm, 256), 1) % 128
    first_half = jnp.logical_and(lane2 >= NOPE_C, lane2 < NOPE_C + half)
    second_half = jnp.logical_and(lane2 >= NOPE_C + half, lane2 < NOPE_C + ROPE_C)
    for c in range(H_C * HEAD_PAD // 256):
        cols = slice(c * 256, (c + 1) * 256)
        zq = _dot(cqb, wq_ref[:, cols])
        qn = zq * lax.rsqrt(_dot((zq * zq).astype(BF16), gm) + EPS) * gq_ref[:, cols]
        rot_q = jnp.where(first_half, -pltpu.roll(qn, 256 - half, 1),
                          jnp.where(second_half, pltpu.roll(qn, half, 1), 0.0))
        qh_ref[:, cols] = (qn * cos2 + rot_q * sin2).astype(BF16)
        zk = _dot(ckvb, wk_ref[:, cols])
        kn = zk * lax.rsqrt(_dot((zk * zk).astype(BF16), gm) + EPS) * gk_ref[:, cols] + kr2
        kh_ref[:, cols] = kn.astype(BF16)
    v_ref[...] = _dot(ckvb, wv_ref[...]).astype(BF16)


def _mla_proj(x, g, win, qag, kvg, rkg, wq, gq, gm, wk, gk, wv, tabs, *, tab_tiles, tm):
    n, d = x.shape
    tm = _row_tile(n, tm)
    hw = H_C * HEAD_PAD
    row = lambda c: pl.BlockSpec((tm, c), lambda i: (i, 0))
    tab = pl.BlockSpec((tm, 128), lambda i: (i % tab_tiles, 0))
    return pl.pallas_call(
        functools.partial(_mla_proj_kernel, tm=tm),
        grid=(n // tm,),
        in_specs=[row(d), _const((1, d)), _const((d, 512)), _const((1, Q_RANK)), _const((1, KV_RANK)),
                  _const((1, 128)), _const((Q_RANK, hw)), _const((1, hw)), _const((256, 256)),
                  _const((KV_RANK, hw)), _const((1, hw)), _const((KV_RANK, H_C * V_C)),
                  tab, tab, tab, tab],
        out_specs=[row(hw), row(hw), row(H_C * V_C), row(LAT_C)],
        out_shape=[jax.ShapeDtypeStruct((n, hw), BF16), jax.ShapeDtypeStruct((n, hw), BF16),
                   jax.ShapeDtypeStruct((n, H_C * V_C), BF16), jax.ShapeDtypeStruct((n, LAT_C), F32)],
        compiler_params=_params("parallel"),
        name="mla_proj",
    )(x, g, win, qag, kvg, rkg, wq, gq, gm, wk, gk, wv, *tabs)


def _diff_sample_kernel(pt_ref, qbd_ref, knew_ref, lam_ref, sg_ref, *refs, n_pages_step, n_steps, lam_init):
    del pt_ref
    page_refs = refs[:n_pages_step]
    o_ref, m_s, l_s, acc_s = refs[n_pages_step:]
    c = pl.program_id(1)

    @pl.when(c == 0)
    def _():
        m_s[...] = jnp.full_like(m_s, NEG_BIG)
        l_s[...] = jnp.zeros_like(l_s)
        acc_s[...] = jnp.zeros_like(acc_s)

    def gather(pr, first_row):
        return jnp.concatenate([pr[pl.ds(first_row + h, PAGE, stride=8), :] for h in range(H_A)],
                               axis=1).astype(BF16)

    qf = qbd_ref[0]
    qb = qf.astype(BF16)
    s = jnp.concatenate([_nt_dot(qb, gather(pr, 0)) for pr in page_refs], axis=1)
    m_old = m_s[...]
    m_new = jnp.maximum(m_old, jnp.max(s, axis=-1, keepdims=True))
    alpha = jnp.exp(m_old - m_new)
    p = jnp.exp(s - m_new)
    l_s[...] = alpha * l_s[...] + jnp.sum(p, axis=-1, keepdims=True)
    pb = p.astype(BF16)
    pv = _dot(pb[:, 0:PAGE], gather(page_refs[0], H_A))
    for i in range(1, n_pages_step):
        pv = pv + _dot(pb[:, i * PAGE:(i + 1) * PAGE], gather(page_refs[i], H_A))
    acc_s[...] = alpha * acc_s[...] + pv
    m_s[...] = m_new

    @pl.when(c == n_steps - 1)
    def _():
        row = lax.broadcasted_iota(jnp.int32, (32, 1), 0)
        qidx = row % 4
        m0 = m_s[...]
        sj = []
        for j in range(4):
            kj = knew_ref[0, j:j + 1, 0:512]
            sj.append(jnp.where(j <= qidx, jnp.sum(qf * kj, axis=-1, keepdims=True), NEG_BIG))
        m1 = jnp.maximum(jnp.maximum(jnp.maximum(m0, sj[0]), jnp.maximum(sj[1], sj[2])), sj[3])
        al = jnp.exp(m0 - m1)
        l1 = al * l_s[...]
        acc = al * acc_s[...]
        for j in range(4):
            pj = jnp.exp(sj[j] - m1)
            l1 = l1 + pj
            acc = acc + pj * knew_ref[0, j:j + 1, 512:1024]
        accn = acc * (1.0 / l1)
        lv = lam_ref[...]
        lam = (jnp.exp(jnp.sum(lv[0:1] * lv[1:2], axis=-1, keepdims=True))
               - jnp.exp(jnp.sum(lv[2:3] * lv[3:4], axis=-1, keepdims=True)) + lam_init)
        o = accn[0:16] - lam * accn[16:32]
        lane_head = lax.broadcasted_iota(jnp.int32, (16, 512), 1) // 128
        row_head = lax.broadcasted_iota(jnp.int32, (16, 512), 0) // 4
        o = jnp.where(lane_head == row_head, o, 0.0)
        o = (o * lax.rsqrt(jnp.sum(o * o, axis=-1, keepdims=True) * (1.0 / 128.0) + EPS)
             * sg_ref[...] * (1.0 - lam_init))
        o_ref[0] = o[0:4] + o[4:8] + o[8:12] + o[12:16]


def _diff_sample(page_table, qbd, knew, lam, sg, cache, *, n_pages_step, lam_init):
    db, n_pages = page_table.shape
    n_steps = n_pages // n_pages_step
    page_specs = [pl.BlockSpec((PAGE * 8, 128), lambda s, c, pt, i=i: (pt[s, c * n_pages_step + i], 0))
                  for i in range(n_pages_step)]
    grid_spec = pltpu.PrefetchScalarGridSpec(
        num_scalar_prefetch=1,
        grid=(db, n_steps),
        in_specs=[pl.BlockSpec((1, 32, 512), lambda s, c, pt: (s, 0, 0)),
                  pl.BlockSpec((1, 8, 1024), lambda s, c, pt: (s, 0, 0)),
                  pl.BlockSpec((4, HD_A), lambda s, c, pt: (0, 0)),
                  pl.BlockSpec((1, 512), lambda s, c, pt: (0, 0))] + page_specs,
        out_specs=pl.BlockSpec((1, 4, 512), lambda s, c, pt: (s, 0, 0)),
        scratch_shapes=[pltpu.VMEM((32, 1), F32), pltpu.VMEM((32, 1), F32), pltpu.VMEM((32, 512), F32)],
    )
    return pl.pallas_call(
        functools.partial(_diff_sample_kernel, n_pages_step=n_pages_step, n_steps=n_steps, lam_init=lam_init),
        grid_spec=grid_spec,
        out_shape=jax.ShapeDtypeStruct((db, 4, 512), F32),
        compiler_params=_params("arbitrary", "arbitrary"),
        name="diff_sample",
    )(page_table, qbd, knew, lam, sg, *([cache] * n_pages_step))


def _mla_sample_kernel(pt_ref, qbd_ref, qrope_ref, qrows_ref, knew_ref, cnew_ref, wkt_ref, wkg_ref, wv_ref,
                       *refs, n_pages_step, n_steps):
    del pt_ref
    page_refs = refs[:n_pages_step]
    o_ref, m_s, l_s, lat_s, qabs_s = refs[n_pages_step:]
    c = pl.program_id(1)
    nk = n_pages_step * PAGE

    @pl.when(c == 0)
    def _():
        m_s[...] = jnp.full_like(m_s, NEG_BIG)
        l_s[...] = jnp.zeros_like(l_s)
        lat_s[...] = jnp.zeros_like(lat_s)
        qabs_s[...] = _dot(qbd_ref[0], wkg_ref[...]).astype(BF16)

    ct = jnp.concatenate([pr[0:KV_RANK, :] for pr in page_refs], axis=1).astype(BF16)
    krt = jnp.concatenate([pr[KV_RANK:LAT_C, :] for pr in page_refs], axis=1)
    krt = jnp.concatenate([krt, jnp.zeros((128 - ROPE_C, nk), F32)], axis=0).astype(BF16)
    kpt = _dot(wkt_ref[...], ct)
    msq = jnp.sum((kpt * kpt).reshape(H_C, NOPE_C, nk), axis=1) * (1.0 / NOPE_C)
    rinv = lax.rsqrt(msq + EPS)
    s = (_dot(qabs_s[...], ct) * jnp.concatenate([rinv] * 4, axis=0)
         + _dot(qrope_ref[0], krt))
    m_old = m_s[...]
    m_new = jnp.maximum(m_old, jnp.max(s, axis=-1, keepdims=True))
    alpha = jnp.exp(m_old - m_new)
    p = jnp.exp(s - m_new)
    l_s[...] = alpha * l_s[...] + jnp.sum(p, axis=-1, keepdims=True)
    lat_s[...] = alpha * lat_s[...] + _nt_dot(p.astype(BF16), ct)
    m_s[...] = m_new

    @pl.when(c == n_steps - 1)
    def _():
        qrows = qrows_ref[0]
        qidx = lax.broadcasted_iota(jnp.int32, (64, 1), 0) // H_C
        m0 = m_s[...]
        sj = []
        for j in range(4):
            kt = jnp.concatenate([knew_ref[0, j]] * 4, axis=0)
            sj.append(jnp.where(j <= qidx, jnp.sum(qrows * kt, axis=-1, keepdims=True), NEG_BIG))
        m1 = jnp.maximum(jnp.maximum(jnp.maximum(m0, sj[0]), jnp.maximum(sj[1], sj[2])), sj[3])
        al = jnp.exp(m0 - m1)
        l1 = al * l_s[...]
        lat = al * lat_s[...]
        for j in range(4):
            pj = jnp.exp(sj[j] - m1)
            l1 = l1 + pj
            lat = lat + pj * cnew_ref[0, j:j + 1, :]
        latn = (lat * (1.0 / l1)).astype(BF16)
        full = _dot(latn, wv_ref[...])
        lane_head = lax.broadcasted_iota(jnp.int32, (64, H_C * V_C), 1) // V_C
        row_head = lax.broadcasted_iota(jnp.int32, (64, H_C * V_C), 0) % H_C
        full = jnp.where(lane_head == row_head, full, 0.0)
        for q in range(4):
            o_ref[0, q:q + 1, :] = jnp.sum(full[q * H_C:(q + 1) * H_C], axis=0, keepdims=True)
        o_ref[0, 4:8, :] = jnp.zeros((4, H_C * V_C), F32)


def _mla_sample(page_table, qbd, qrope, qrows, knew, cnew, wkt, wkg, wv, cache, *, n_pages_step):
    db, n_pages = page_table.shape
    n_steps = n_pages // n_pages_step
    seq3 = lambda a, b: pl.BlockSpec((1, a, b), lambda s, c, pt: (s, 0, 0))
    cst = lambda a, b: pl.BlockSpec((a, b), lambda s, c, pt: (0, 0))
    page_specs = [pl.BlockSpec((LAT_C, PAGE), lambda s, c, pt, i=i: (pt[s, c * n_pages_step + i], 0))
                  for i in range(n_pages_step)]
    grid_spec = pltpu.PrefetchScalarGridSpec(
        num_scalar_prefetch=1,
        grid=(db, n_steps),
        in_specs=[seq3(64, H_C * NOPE_C), seq3(64, 128), seq3(64, 128),
                  pl.BlockSpec((1, 4, H_C, 128), lambda s, c, pt: (s, 0, 0, 0)),
                  seq3(8, KV_RANK), cst(H_C * NOPE_C, KV_RANK), cst(H_C * NOPE_C, KV_RANK),
                  cst(KV_RANK, H_C * V_C)] + page_specs,
        out_specs=seq3(8, H_C * V_C),
        scratch_shapes=[pltpu.VMEM((64, 1), F32), pltpu.VMEM((64, 1), F32), pltpu.VMEM((64, KV_RANK), F32),
                        pltpu.VMEM((64, KV_RANK), BF16)],
    )
    return pl.pallas_call(
        functools.partial(_mla_sample_kernel, n_pages_step=n_pages_step, n_steps=n_steps),
        grid_spec=grid_spec,
        out_shape=jax.ShapeDtypeStruct((db, 8, H_C * V_C), F32),
        compiler_params=_params("arbitrary", "arbitrary"),
        name="mla_sample",
    )(page_table, qbd, qrope, qrows, knew, cnew, wkt, wkg, wv, *([cache] * n_pages_step))


def _group_mean_matrix(sizes, total):
    m = jnp.zeros((total, total), F32)
    off = 0
    for sz, live in sizes:
        if live:
            m = m.at[off:off + sz, off:off + sz].set(1.0 / sz)
        off += sz
    return m.astype(BF16)


def _rope_tables(pos, lane_offset):
    half = ROPE_C // 2
    inv = ROPE_THETA ** (-jnp.arange(half, dtype=F32) / half)
    ang = pos.astype(F32)[:, None] * inv[None, :]
    c2 = jnp.concatenate([jnp.cos(ang), jnp.cos(ang)], axis=1)
    s2 = jnp.concatenate([jnp.sin(ang), jnp.sin(ang)], axis=1)
    cos = jnp.ones((pos.shape[0], 128), F32).at[:, lane_offset:lane_offset + ROPE_C].set(c2)
    sin = jnp.zeros((pos.shape[0], 128), F32).at[:, lane_offset:lane_offset + ROPE_C].set(s2)
    return cos, sin


def kernel(x_prompt, x_sample, cache_kv_diff, cache_mla, state_lru_h, state_lru_conv, state_ffn_conv, page_table,
           norm_mix, norm_ffn, w_in_e, qk_norm_a, lambda_a, subln_a, conv_w_lru, conv_b_lru, gate_w_lru,
           gate_b_lru, lambda_lru, w_out_e, w_in_c, q_a_norm, kv_a_norm, w_uq, w_ukv, qn_c, rn_c, w_out_c,
           w_up, conv_w_ffn, conv_b_ffn, w_down):
    B, T, D = x_prompt.shape
    DB, S, _ = x_sample.shape
    n_pages = page_table.shape[1]
    n_past = n_pages * PAGE
    n_pool = cache_kv_diff.shape[1]
    depth = norm_mix.shape[0]
    NP, NS = B * T, S * DB
    pt = page_table.astype(jnp.int32)

    xp = x_prompt.reshape(NP, D)
    xs = jnp.swapaxes(x_sample, 0, 1).reshape(NS, D)

    def to_seq_major(a):
        return jnp.swapaxes(a.reshape((S, DB) + a.shape[1:]), 0, 1)

    pos_p = jnp.arange(T, dtype=jnp.int32)
    pos_s = jnp.repeat(n_past + jnp.arange(S, dtype=jnp.int32), DB)
    tabs_p = _rope_tables(pos_p, NOPE_C) + _rope_tables(pos_p, 0)
    tabs_s = _rope_tables(pos_s, NOPE_C) + _rope_tables(pos_s, 0)

    kv_p, kv_s, mla_p, mla_s = [], [], [], []
    lh_p, lh_s, lc_p, lc_s, fc_p, fc_s = [], [], [], [], [], []

    for l in range(depth):
        g_mix = norm_mix[l].reshape(1, D)
        if l % 2 == 0:
            e = l // 2
            lam_init = 0.8 - 0.6 * math.exp(-0.3 * l)
            w_in = w_in_e[e].astype(BF16)
            qs = jnp.tile(qk_norm_a[e, 0].reshape(1, 2 * HD_A), (1, H_A)) * (1.0 / math.sqrt(HD_A))
            ks = jnp.tile(qk_norm_a[e, 1].reshape(1, 2 * HD_A), (1, H_A))
            gm64 = _group_mean_matrix([(HD_A, True)] * (2 * H_A), 2 * H_A * HD_A)
            lam_v = lambda_a[e]
            sg = subln_a[e].reshape(1, 2 * HD_A)
            cw, cb = conv_w_lru[e], conv_b_lru[e].reshape(1, D_RNN)
            gw = jnp.zeros((2, 8, 64, 8, 64), F32)
            gw = gw.at[:, jnp.arange(8), :, jnp.arange(8), :].set(jnp.swapaxes(gate_w_lru[e], 0, 1))
            gw = jnp.concatenate([gw[0].reshape(D_RNN, D_RNN), gw[1].reshape(D_RNN, D_RNN)], axis=1).astype(BF16)
            gb = gate_b_lru[e].reshape(1, 2 * D_RNN)
            lam_lru = lambda_lru[e].reshape(1, D_RNN)
            w_out = w_out_e[e].astype(BF16)

            qb_p, kb_p, vb_p, kvf_p, xg_p = _even_proj(xp, g_mix, w_in, qs, ks, gm64, 512)
            qb_s, kb_s, vb_s, kvf_s, xg_s = _even_proj(xs, g_mix, w_in, qs, ks, gm64, 512)

            a_p = _flash(qb_p, kb_p, vb_p, (lam_v, sg), diff=True, batch=B, seq=T, tb=512, npair=2, lam_init=lam_init)

            q_sm = to_seq_major(qb_s.astype(F32)).reshape(DB, S, H_A, 2, HD_A)
            eye_h = jnp.eye(H_A, dtype=F32)
            eye_m = jnp.eye(2, dtype=F32)
            qbd = jnp.einsum('bqhmd,hg,mn->bmhqgnd', q_sm, eye_h, eye_m)
            qbd = qbd.reshape(DB, 2 * H_A * S, H_A * 2 * HD_A)
            knew = jnp.pad(to_seq_major(kvf_s), ((0, 0), (0, 8 - S), (0, 0)))
            cache_d = cache_kv_diff.reshape(cache_kv_diff.shape[0] * n_pool * PAGE * 2 * H_A, 2 * HD_A)
            sg4 = jnp.tile(sg, (1, H_A))
            a_s = _diff_sample(pt + e * n_pool, qbd, knew, lam_v, sg4, cache_d,
                               n_pages_step=min(16, n_pages), lam_init=lam_init)
            a_s = jnp.swapaxes(a_s, 0, 1).reshape(NS, H_A * 2 * HD_A).astype(BF16)

            y_p, hl_pp, ct_pp = _lru_prompt(xg_p, cw, cb, gw, gb, lam_lru, batch=B, seq=T, tm=256)
            hist = jnp.swapaxes(state_lru_conv[e], 0, 1).reshape(3 * DB, D_RNN)
            y_s, hl_ss, ct_ss = _lru_sample(xg_s, hist, state_lru_h[e], cw, cb, gw, gb, lam_lru, db=DB, steps=S)

            xp = _mm_res(xp, [a_p, y_p], w_out, 512)
            xs = _mm_res(xs, [a_s, y_s], w_out, 512)

            kv_p.append(kvf_p.reshape(B, T, 2, H_A, 2 * HD_A))
            kv_s.append(to_seq_major(kvf_s).reshape(DB, S, 2, H_A, 2 * HD_A))
            nt = hl_pp.shape[0] // B
            lh_p.append(hl_pp.reshape(B, nt, 8, D_RNN)[:, -1, -1])
            lc_p.append(ct_pp.reshape(B, nt, 8, D_RNN)[:, -1, 5:8])
            lh_s.append(hl_ss)
            lc_s.append(jnp.swapaxes(ct_ss.reshape(3, DB, D_RNN), 0, 1))
        else:
            o = l // 2
            scale = 1.0 / math.sqrt(NOPE_C + ROPE_C)
            pad_h = HEAD_PAD - NOPE_C - ROPE_C
            win = jnp.pad(w_in_c[o], ((0, 0), (0, 512 - w_in_c.shape[2]))).astype(BF16)
            qag = q_a_norm[o].reshape(1, Q_RANK)
            kvg = kv_a_norm[o].reshape(1, KV_RANK)
            rkg = jnp.pad(rn_c[o, 1], (0, 128 - ROPE_C)).reshape(1, 128)
            wq = jnp.pad(w_uq[o], ((0, 0), (0, 0), (0, pad_h))).reshape(Q_RANK, H_C * HEAD_PAD).astype(BF16)
            gq = jnp.tile(jnp.concatenate([qn_c[o, 0], rn_c[o, 0], jnp.zeros((pad_h,), F32)]) * scale,
                          H_C).reshape(1, H_C * HEAD_PAD)
            gm256 = _group_mean_matrix([(NOPE_C, True), (ROPE_C, True), (pad_h, False)] * 2, 256)
            wk_nope = w_ukv[o][:, :, :NOPE_C]
            wk = jnp.pad(wk_nope, ((0, 0), (0, 0), (0, HEAD_PAD - NOPE_C))).reshape(KV_RANK, H_C * HEAD_PAD)
            wk = wk.astype(BF16)
            gk = jnp.tile(jnp.concatenate([qn_c[o, 1], jnp.zeros((HEAD_PAD - NOPE_C,), F32)]),
                          H_C).reshape(1, H_C * HEAD_PAD)
            wv = w_ukv[o][:, :, NOPE_C:].reshape(KV_RANK, H_C * V_C).astype(BF16)
            wkt = jnp.transpose(wk_nope.reshape(KV_RANK, H_C * NOPE_C))
            wkg = (wkt * jnp.tile(qn_c[o, 1], H_C)[:, None]).astype(BF16)
            wkt = wkt.astype(BF16)
            w_out = w_out_c[o].astype(BF16)

            tm_p = _row_tile(T, 512)
            qh_p, kh_p, v_p, rows_p = _mla_proj(xp, g_mix, win, qag, kvg, rkg, wq, gq, gm256, wk, gk, wv, tabs_p,
                                                tab_tiles=T // tm_p, tm=tm_p)
            qh_s, kh_s, v_s, rows_s = _mla_proj(xs, g_mix, win, qag, kvg, rkg, wq, gq, gm256, wk, gk, wv, tabs_s,
                                                tab_tiles=1, tm=NS)

            o_p = _flash(qh_p, kh_p, v_p, (), diff=False, batch=B, seq=T, tb=512, npair=2)

            qh_sm = to_seq_major(qh_s.astype(F32)).reshape(DB, S, H_C, HEAD_PAD)
            qrows = qh_sm.reshape(DB, S * H_C, HEAD_PAD)
            qn = qh_sm[..., :NOPE_C]
            qbd = jnp.einsum('bqhd,hg->bqhgd', qn, jnp.eye(H_C, dtype=F32)).reshape(DB, S * H_C, H_C * NOPE_C)
            qrope = jnp.pad(qh_sm[..., NOPE_C:NOPE_C + ROPE_C],
                            ((0, 0), (0, 0), (0, 0), (0, 128 - ROPE_C))).reshape(DB, S * H_C, 128)
            knew = to_seq_major(kh_s.astype(F32)).reshape(DB, S, H_C, HEAD_PAD)
            rows_sm = to_seq_major(rows_s)
            cnew = jnp.pad(rows_sm[..., :KV_RANK], ((0, 0), (0, 8 - S), (0, 0)))
            cache_m = jnp.swapaxes(cache_mla, 2, 3).reshape(cache_mla.shape[0] * n_pool * LAT_C, PAGE)
            o_s = _mla_sample(pt + o * n_pool, qbd.astype(BF16), qrope.astype(BF16), qrows, knew, cnew,
                              wkt, wkg, wv, cache_m, n_pages_step=min(16, n_pages))
            o_s = jnp.swapaxes(o_s[:, :S], 0, 1).reshape(NS, H_C * V_C).astype(BF16)

            xp = _mm_res(xp, [o_p], w_out, 512)
            xs = _mm_res(xs, [o_s], w_out, 512)
            mla_p.append(rows_p.reshape(B, T, LAT_C))
            mla_s.append(rows_sm)

        g_ffn = norm_ffn[l].reshape(1, D)
        wu = w_up[l].astype(BF16)
        wd = w_down[l].astype(BF16)
        cwf, cbf = conv_w_ffn[l], conv_b_ffn[l].reshape(1, 2 * D_FF)
        xp, tg, tl = _ffn_prompt(xp, g_ffn, wu, cwf, cbf, wd, seq=T, tm=512, tf=1408)
        ntf = tg.shape[0] // B
        tail = jnp.concatenate([tg, tl], axis=-1).reshape(B, ntf, 8, 2 * D_FF)
        fc_p.append(tail[:, -1, 6:8])
        hist_f = jnp.swapaxes(state_ffn_conv[l], 0, 1).reshape(2 * DB, 2 * D_FF)
        xs, tgs, tls = _ffn_sample(xs, g_ffn, wu, cwf, cbf, wd, hist_f, db=DB, steps=S, tf=1408)
        fc_s.append(jnp.swapaxes(jnp.concatenate([tgs, tls], axis=-1).reshape(2, DB, 2 * D_FF), 0, 1))

    y_p = xp.reshape(B, T, D)
    y_s = to_seq_major(xs)
    return (y_p, y_s, jnp.stack(kv_p), jnp.stack(kv_s), jnp.stack(mla_p), jnp.stack(mla_s),
            jnp.stack(lh_p), jnp.stack(lh_s), jnp.stack(lc_p), jnp.stack(lc_s), jnp.stack(fc_p), jnp.stack(fc_s))
```

```python
import functools
import math

import jax
import jax.numpy as jnp
from jax import lax
from jax.experimental import pallas as pl
from jax.experimental.pallas import tpu as pltpu

F32 = jnp.float32
BF16 = jnp.bfloat16

EPS = 1e-6
NEG_BIG = -1e30
ROPE_THETA = 10000.0
LRU_C = 8.0
PAGE = 128

H_A = 4
HD_A = 64
D_RNN = 512
H_C = 16
Q_RANK = 256
KV_RANK = 128
NOPE_C = 64
ROPE_C = 32
V_C = 64
LAT_C = KV_RANK + ROPE_C
HEAD_PAD = 128
D_FF = 2816

VMEM_LIMIT_BYTES = 56 * 1024 * 1024


def _params(*semantics):
    return pltpu.CompilerParams(dimension_semantics=semantics,
                                vmem_limit_bytes=VMEM_LIMIT_BYTES)


def _row_tile(n, want):
    t = min(n, want)
    assert n % t == 0
    return t


def _const(shape):
    return pl.BlockSpec(shape, lambda *_: (0,) * len(shape))


def _rms(x, g):
    return x * lax.rsqrt(jnp.mean(x * x, axis=-1, keepdims=True) + EPS) * g


def _nt_dot(a, b):
    return lax.dot_general(a, b, (((1,), (1,)), ((), ())), preferred_element_type=F32)


def _dot(a, b):
    return jnp.dot(a, b, preferred_element_type=F32)


def _gelu_tanh(x):
    return 0.5 * x * (1.0 + jnp.tanh(math.sqrt(2.0 / math.pi) * (x + 0.044715 * x * x * x)))


def _softplus(y):
    return jnp.maximum(y, 0.0) + jnp.log1p(jnp.exp(-jnp.abs(y)))


def _shift_rows(x, prev8, s):
    xr = pltpu.roll(x, s, 0)
    row8 = lax.broadcasted_iota(jnp.int32, prev8.shape, 0)
    top = jnp.where(row8 < s, pltpu.roll(prev8, s, 0), xr[0:8])
    return jnp.concatenate([top, xr[8:]], axis=0)


def _even_proj_kernel(x_ref, g_ref, w_ref, qs_ref, ks_ref, gm_ref,
                      qb_ref, kb_ref, vb_ref, kv_ref, xg_ref):
    hb = _rms(x_ref[...], g_ref[...]).astype(BF16)
    z = _dot(hb, w_ref[...])
    zq, zk, zv = z[:, 0:512], z[:, 512:1024], z[:, 1024:1536]
    gm = gm_ref[...]
    mq = _dot((zq * zq).astype(BF16), gm)
    mk = _dot((zk * zk).astype(BF16), gm)
    q = zq * lax.rsqrt(mq + EPS) * qs_ref[...]
    k = zk * lax.rsqrt(mk + EPS) * ks_ref[...]
    qb_ref[...] = q.astype(BF16)
    kb_ref[...] = k.astype(BF16)
    vb_ref[...] = zv.astype(BF16)
    kv_ref[:, 0:512] = k
    kv_ref[:, 512:1024] = zv
    xg_ref[...] = z[:, 1536:2560]


def _even_proj(x, g, w, qs, ks, gm, tm):
    n = x.shape[0]
    tm = _row_tile(n, tm)
    row = lambda c: pl.BlockSpec((tm, c), lambda i: (i, 0))
    return pl.pallas_call(
        _even_proj_kernel,
        grid=(n // tm,),
        in_specs=[row(1024), _const((1, 1024)), _const((1024, 2560)), _const((1, 512)),
                  _const((1, 512)), _const((512, 512))],
        out_specs=[row(512), row(512), row(512), row(1024), row(1024)],
        out_shape=[jax.ShapeDtypeStruct((n, 512), BF16), jax.ShapeDtypeStruct((n, 512), BF16),
                   jax.ShapeDtypeStruct((n, 512), BF16), jax.ShapeDtypeStruct((n, 1024), F32),
                   jax.ShapeDtypeStruct((n, 1024), F32)],
        compiler_params=_params("parallel"),
        name="even_proj",
    )(x, g, w, qs, ks, gm)


def _online_update(s, m, acc, vext):
    m_new = jnp.maximum(m, jnp.max(s, axis=-1, keepdims=True))
    alpha = jnp.exp(m - m_new)
    p = jnp.exp(s - m_new).astype(BF16)
    return m_new, alpha * acc + _dot(p, vext)


def _flash_kernel(*refs, diff, tb, npair, lam_init):
    if diff:
        q_ref, k_ref, v_ref, lam_ref, sg_ref, o_ref = refs
    else:
        q_ref, k_ref, v_ref, o_ref = refs
    qi = pl.program_id(2)
    qw = 128 if diff else 256
    nchain = 2 * npair
    lane = lax.broadcasted_iota(jnp.int32, (tb, 128), 1)
    qs = []
    for g in range(npair):
        q = q_ref[:, g * qw:(g + 1) * qw]
        if diff:
            zero = jnp.zeros_like(q)
            qs.append(jnp.where(lane < HD_A, q, zero))
            qs.append(jnp.where(lane >= HD_A, q, zero))
        else:
            qs += [q[:, 0:128], q[:, 128:256]]

    ones = jnp.ones((tb, 128), BF16)

    def scores(j):
        start = pl.multiple_of(j * tb, tb)
        out = []
        for g in range(npair):
            kblk = k_ref[pl.ds(start, tb), g * qw:(g + 1) * qw]
            ks = [kblk, kblk] if diff else [kblk[:, 0:128], kblk[:, 128:256]]
            out += [_nt_dot(qs[2 * g + c], ks[c]) for c in range(2)]
        return out

    def values(j):
        start = pl.multiple_of(j * tb, tb)
        out = []
        for g in range(npair):
            v = v_ref[pl.ds(start, tb), g * 128:(g + 1) * 128]
            if diff:
                vext = jnp.concatenate([v, ones], axis=1)
                out += [vext, vext]
            else:
                out += [jnp.where(lane < V_C, v, ones), jnp.where(lane < V_C, ones, v)]
        return out

    def body(j, carry):
        s, vs = scores(j), values(j)
        out = []
        for c in range(nchain):
            out += list(_online_update(s[c], carry[2 * c], carry[2 * c + 1], vs[c]))
        return tuple(out)

    col1 = jnp.full((tb, 1), NEG_BIG, F32)
    zacc = jnp.zeros((tb, 256 if diff else 128), F32)
    carry = lax.fori_loop(0, qi, body, (col1, zacc) * nchain)

    s, vs = scores(qi), values(qi)
    r = lax.broadcasted_iota(jnp.int32, (tb, tb), 0)
    c = lax.broadcasted_iota(jnp.int32, (tb, tb), 1)
    keep = c <= r
    for g in range(npair):
        os_ = []
        for c2 in (2 * g, 2 * g + 1):
            _, a = _online_update(jnp.where(keep, s[c2], NEG_BIG), carry[2 * c2], carry[2 * c2 + 1], vs[c2])
            if diff:
                os_.append(a[:, 0:128] * (1.0 / a[:, 128:129]))
            else:
                den = a[:, 64:65] if c2 % 2 == 0 else a[:, 0:1]
                os_.append(a * (1.0 / den))
        if diff:
            lv = lam_ref[...]
            lam = (jnp.exp(jnp.sum(lv[0:1] * lv[1:2], axis=-1, keepdims=True))
                   - jnp.exp(jnp.sum(lv[2:3] * lv[3:4], axis=-1, keepdims=True)) + lam_init)
            o = os_[0] - lam * os_[1]
            o = _rms(o, sg_ref[...]) * (1.0 - lam_init)
        else:
            o = jnp.where(lane < V_C, os_[0], os_[1])
        o_ref[:, g * 128:(g + 1) * 128] = o.astype(o_ref.dtype)


def _flash(q, k, v, extra, *, diff, batch, seq, tb, npair, lam_init=0.0):
    n = q.shape[0]
    tb = _row_tile(seq, tb)
    nq = seq // tb
    qw = (128 if diff else 256) * npair
    groups = q.shape[1] // qw
    in_specs = [pl.BlockSpec((tb, qw), lambda b, g, i: (b * nq + i, g)),
                pl.BlockSpec((seq, qw), lambda b, g, i: (b, g)),
                pl.BlockSpec((seq, 128 * npair), lambda b, g, i: (b, g))]
    in_specs += [_const(e.shape) for e in extra]
    return pl.pallas_call(
        functools.partial(_flash_kernel, diff=diff, tb=tb, npair=npair, lam_init=lam_init),
        grid=(batch, groups, nq),
        in_specs=in_specs,
        out_specs=pl.BlockSpec((tb, 128 * npair), lambda b, g, i: (b * nq + i, g)),
        out_shape=jax.ShapeDtypeStruct((n, groups * 128 * npair), BF16),
        compiler_params=_params("parallel", "parallel", "arbitrary"),
        name="flash_diff" if diff else "flash_mla",
    )(q, k, v, *extra)


def _lru_gates(xc, gw_ref, gb_ref, lam_ref):
    g = _dot(xc.astype(BF16), gw_ref[...]) + gb_ref[...]
    r = jax.nn.sigmoid(g[:, 0:D_RNN])
    i = jax.nn.sigmoid(g[:, D_RNN:2 * D_RNN])
    a = jnp.exp(-LRU_C * r * _softplus(-lam_ref[...]))
    mult = jnp.sqrt(1.0 - a * a)
    return a, mult, i


def _lru_prompt_kernel(x_ref, gr_ref, cw_ref, cb_ref, gw_ref, gb_ref, lam_ref,
                       y_ref, hl_ref, ct_ref, xprev_s, h_s, *, tm):
    ti = pl.program_id(1)

    @pl.when(ti == 0)
    def _():
        xprev_s[...] = jnp.zeros_like(xprev_s)
        h_s[...] = jnp.zeros_like(h_s)

    x = x_ref[...]
    prev8 = xprev_s[...]
    cw = cw_ref[...]
    xc = (cb_ref[...] + cw[0:1] * _shift_rows(x, prev8, 3) + cw[1:2] * _shift_rows(x, prev8, 2)
          + cw[2:3] * _shift_rows(x, prev8, 1) + cw[3:4] * x)
    a, mult, i = _lru_gates(xc, gw_ref, gb_ref, lam_ref)
    row = lax.broadcasted_iota(jnp.int32, (tm, D_RNN), 0)
    mult = jnp.where(jnp.logical_and(row == 0, ti == 0), 1.0, mult)
    b = mult * i * xc
    d = 1
    while d < tm:
        ar = jnp.where(row >= d, pltpu.roll(a, d, 0), 1.0)
        br = jnp.where(row >= d, pltpu.roll(b, d, 0), 0.0)
        b = a * br + b
        a = a * ar
        d *= 2
    h = a * h_s[7:8] + b
    y_ref[...] = (h * _gelu_tanh(gr_ref[...])).astype(BF16)
    h_s[...] = h[tm - 8:tm]
    hl_ref[0] = h[tm - 8:tm]
    ct_ref[0] = x[tm - 8:tm]
    xprev_s[...] = x[tm - 8:tm]


def _lru_prompt(xg, cw, cb, gw, gb, lam, *, batch, seq, tm):
    n = xg.shape[0]
    tm = _row_tile(seq, tm)
    nt = seq // tm
    return pl.pallas_call(
        functools.partial(_lru_prompt_kernel, tm=tm),
        grid=(batch, nt),
        in_specs=[pl.BlockSpec((tm, D_RNN), lambda b, t: (b * nt + t, 0)),
                  pl.BlockSpec((tm, D_RNN), lambda b, t: (b * nt + t, 1)),
                  _const((4, D_RNN)), _const((1, D_RNN)), _const((D_RNN, 2 * D_RNN)),
                  _const((1, 2 * D_RNN)), _const((1, D_RNN))],
        out_specs=[pl.BlockSpec((tm, D_RNN), lambda b, t: (b * nt + t, 0)),
                   pl.BlockSpec((1, 8, D_RNN), lambda b, t: (b * nt + t, 0, 0)),
                   pl.BlockSpec((1, 8, D_RNN), lambda b, t: (b * nt + t, 0, 0))],
        out_shape=[jax.ShapeDtypeStruct((n, D_RNN), BF16),
                   jax.ShapeDtypeStruct((batch * nt, 8, D_RNN), F32),
                   jax.ShapeDtypeStruct((batch * nt, 8, D_RNN), F32)],
        scratch_shapes=[pltpu.VMEM((8, D_RNN), F32), pltpu.VMEM((8, D_RNN), F32)],
        compiler_params=_params("arbitrary", "arbitrary"),
        name="lru_prompt",
    )(xg, xg, cw, cb, gw, gb, lam)


def _lru_sample_kernel(x_ref, gr_ref, hist_ref, h0_ref, cw_ref, cb_ref, gw_ref, gb_ref, lam_ref,
                       y_ref, hl_ref, ct_ref, *, db, steps):
    n = steps * db
    ext = jnp.concatenate([hist_ref[...], x_ref[...]], axis=0)
    cw = cw_ref[...]
    xc = (cb_ref[...] + cw[0:1] * ext[0:n] + cw[1:2] * ext[db:db + n]
          + cw[2:3] * ext[2 * db:2 * db + n] + cw[3:4] * ext[3 * db:3 * db + n])
    a, mult, i = _lru_gates(xc, gw_ref, gb_ref, lam_ref)
    b = mult * i * xc
    h = h0_ref[...]
    hs = []
    for t in range(steps):
        h = a[t * db:(t + 1) * db] * h + b[t * db:(t + 1) * db]
        hs.append(h)
    hseq = jnp.concatenate(hs, axis=0)
    y_ref[...] = (hseq * _gelu_tanh(gr_ref[...])).astype(BF16)
    hl_ref[...] = h
    ct_ref[...] = ext[n:n + 3 * db]


def _lru_sample(xg, hist, h0, cw, cb, gw, gb, lam, *, db, steps):
    n = steps * db
    return pl.pallas_call(
        functools.partial(_lru_sample_kernel, db=db, steps=steps),
        grid=(1,),
        in_specs=[pl.BlockSpec((n, D_RNN), lambda i: (0, 0)),
                  pl.BlockSpec((n, D_RNN), lambda i: (0, 1)),
                  _const((3 * db, D_RNN)), _const((db, D_RNN)),
                  _const((4, D_RNN)), _const((1, D_RNN)), _const((D_RNN, 2 * D_RNN)),
                  _const((1, 2 * D_RNN)), _const((1, D_RNN))],
        out_specs=[_const((n, D_RNN)), _const((db, D_RNN)), _const((3 * db, D_RNN))],
        out_shape=[jax.ShapeDtypeStruct((n, D_RNN), BF16),
                   jax.ShapeDtypeStruct((db, D_RNN), F32),
                   jax.ShapeDtypeStruct((3 * db, D_RNN), F32)],
        compiler_params=_params("arbitrary"),
        name="lru_sample",
    )(xg, xg, hist, h0, cw, cb, gw, gb, lam)


def _mm_res_kernel(*refs, n_in):
    res_ref, o_ref = refs[0], refs[-1]
    acc = res_ref[...]
    for a_ref, w_ref in zip(refs[1:1 + n_in], refs[1 + n_in:1 + 2 * n_in]):
        acc = acc + _dot(a_ref[...], w_ref[...])
    o_ref[...] = acc


def _mm_res(res, acts, w, tm):
    n, d = res.shape
    tm = _row_tile(n, tm)
    n_in = len(acts)
    kw = w.shape[0] // n_in
    in_specs = [pl.BlockSpec((tm, d), lambda i: (i, 0))]
    in_specs += [pl.BlockSpec((tm, kw), lambda i: (i, 0)) for _ in acts]
    in_specs += [pl.BlockSpec((kw, d), lambda i, j=j: (j, 0)) for j in range(n_in)]
    return pl.pallas_call(
        functools.partial(_mm_res_kernel, n_in=n_in),
        grid=(n // tm,),
        in_specs=in_specs,
        out_specs=pl.BlockSpec((tm, d), lambda i: (i, 0)),
        out_shape=jax.ShapeDtypeStruct((n, d), F32),
        compiler_params=_params("parallel"),
        name="out_proj",
    )(res, *acts, *([w] * n_in))


def _ffn_prompt_kernel(x_ref, g_ref, wg_ref, wl_ref, cwg_ref, cwl_ref, cbg_ref, cbl_ref, wd_ref,
                       o_ref, tg_ref, tl_ref, hb_s, cg_s, cl_s, *, tm, tf, tiles_per_seq):
    i = pl.program_id(0)
    j = pl.program_id(1)

    @pl.when(j == 0)
    def _():
        hb_s[...] = _rms(x_ref[...], g_ref[...]).astype(BF16)
        o_ref[...] = x_ref[...]

    @pl.when((i % tiles_per_seq) == 0)
    def _():
        cg_s[j] = jnp.zeros((8, tf), F32)
        cl_s[j] = jnp.zeros((8, tf), F32)

    hb = hb_s[...]
    prev_g, prev_l = cg_s[j], cl_s[j]
    acts, tails_g, tails_l = [], [], []
    for c0 in range(0, tf, 256):
        cols = slice(c0, min(c0 + 256, tf))

        def conv(w_ref, cw_ref, cb_ref, prev):
            u = _dot(hb, w_ref[:, cols])
            cw = cw_ref[:, cols]
            p8 = prev[:, cols]
            c = cb_ref[:, cols] + cw[0:1] * _shift_rows(u, p8, 2) + cw[1:2] * _shift_rows(u, p8, 1) + cw[2:3] * u
            return c, u[tm - 8:tm]

        cg, tg = conv(wg_ref, cwg_ref, cbg_ref, prev_g)
        cl, tl = conv(wl_ref, cwl_ref, cbl_ref, prev_l)
        tails_g.append(tg)
        tails_l.append(tl)
        acts.append((cg * jax.nn.sigmoid(cg) * cl).astype(BF16))
    tail_g = jnp.concatenate(tails_g, axis=1)
    tail_l = jnp.concatenate(tails_l, axis=1)
    cg_s[j] = tail_g
    cl_s[j] = tail_l
    tg_ref[0] = tail_g
    tl_ref[0] = tail_l
    o_ref[...] += _dot(jnp.concatenate(acts, axis=1), wd_ref[...])


def _ffn_prompt(x, g, w_up, cw, cb, w_down, *, seq, tm, tf):
    n, d = x.shape
    tm = _row_tile(seq, tm)
    nf = D_FF // tf
    nt = n // tm
    return pl.pallas_call(
        functools.partial(_ffn_prompt_kernel, tm=tm, tf=tf, tiles_per_seq=seq // tm),
        grid=(nt, nf),
        in_specs=[pl.BlockSpec((tm, d), lambda i, j: (i, 0)), _const((1, d)),
                  pl.BlockSpec((d, tf), lambda i, j: (0, j)),
                  pl.BlockSpec((d, tf), lambda i, j: (0, j + nf)),
                  pl.BlockSpec((3, tf), lambda i, j: (0, j)),
                  pl.BlockSpec((3, tf), lambda i, j: (0, j + nf)),
                  pl.BlockSpec((1, tf), lambda i, j: (0, j)),
                  pl.BlockSpec((1, tf), lambda i, j: (0, j + nf)),
                  pl.BlockSpec((tf, d), lambda i, j: (j, 0))],
        out_specs=[pl.BlockSpec((tm, d), lambda i, j: (i, 0)),
                   pl.BlockSpec((1, 8, tf), lambda i, j: (i, 0, j)),
                   pl.BlockSpec((1, 8, tf), lambda i, j: (i, 0, j))],
        out_shape=[jax.ShapeDtypeStruct((n, d), F32),
                   jax.ShapeDtypeStruct((nt, 8, D_FF), F32),
                   jax.ShapeDtypeStruct((nt, 8, D_FF), F32)],
        scratch_shapes=[pltpu.VMEM((tm, d), BF16), pltpu.VMEM((nf, 8, tf), F32),
                        pltpu.VMEM((nf, 8, tf), F32)],
        compiler_params=_params("arbitrary", "arbitrary"),
        name="ffn_prompt",
    )(x, g, w_up, w_up, cw, cw, cb, cb, w_down)


def _ffn_sample_kernel(x_ref, g_ref, wg_ref, wl_ref, cwg_ref, cwl_ref, cbg_ref, cbl_ref, wd_ref,
                       hg_ref, hl_ref, o_ref, tg_ref, tl_ref, hb_s, *, db, steps):
    j = pl.program_id(0)
    n = steps * db

    @pl.when(j == 0)
    def _():
        hb_s[...] = _rms(x_ref[...], g_ref[...]).astype(BF16)

    hb = hb_s[...]

    def branch(w_ref, cw_ref, cb_ref, hist_ref, tail_ref):
        u = _dot(hb, w_ref[...])
        ext = jnp.concatenate([hist_ref[...], u], axis=0)
        cw = cw_ref[...]
        c = cb_ref[...] + cw[0:1] * ext[0:n] + cw[1:2] * ext[db:db + n] + cw[2:3] * ext[2 * db:2 * db + n]
        tail_ref[...] = ext[n:n + 2 * db]
        return c

    cg = branch(wg_ref, cwg_ref, cbg_ref, hg_ref, tg_ref)
    cl = branch(wl_ref, cwl_ref, cbl_ref, hl_ref, tl_ref)
    act = (cg * jax.nn.sigmoid(cg) * cl).astype(BF16)
    contrib = _dot(act, wd_ref[...])

    @pl.when(j == 0)
    def _():
        o_ref[...] = x_ref[...] + contrib

    @pl.when(j > 0)
    def _():
        o_ref[...] += contrib


def _ffn_sample(x, g, w_up, cw, cb, w_down, hist, *, db, steps, tf):
    n, d = x.shape
    nf = D_FF // tf
    return pl.pallas_call(
        functools.partial(_ffn_sample_kernel, db=db, steps=steps),
        grid=(nf,),
        in_specs=[_const((n, d)), _const((1, d)),
                  pl.BlockSpec((d, tf), lambda j: (0, j)),
                  pl.BlockSpec((d, tf), lambda j: (0, j + nf)),
                  pl.BlockSpec((3, tf), lambda j: (0, j)),
                  pl.BlockSpec((3, tf), lambda j: (0, j + nf)),
                  pl.BlockSpec((1, tf), lambda j: (0, j)),
                  pl.BlockSpec((1, tf), lambda j: (0, j + nf)),
                  pl.BlockSpec((tf, d), lambda j: (j, 0)),
                  pl.BlockSpec((2 * db, tf), lambda j: (0, j)),
                  pl.BlockSpec((2 * db, tf), lambda j: (0, j + nf))],
        out_specs=[_const((n, d)),
                   pl.BlockSpec((2 * db, tf), lambda j: (0, j)),
                   pl.BlockSpec((2 * db, tf), lambda j: (0, j))],
        out_shape=[jax.ShapeDtypeStruct((n, d), F32),
                   jax.ShapeDtypeStruct((2 * db, D_FF), F32),
                   jax.ShapeDtypeStruct((2 * db, D_FF), F32)],
        scratch_shapes=[pltpu.VMEM((n, d), BF16)],
        compiler_params=_params("arbitrary"),
        name="ffn_sample",
    )(x, g, w_up, w_up, cw, cw, cb, cb, w_down, hist, hist)


def _mla_proj_kernel(x_ref, g_ref, win_ref, qag_ref, kvg_ref, rkg_ref, wq_ref, gq_ref, gm_ref,
                     wk_ref, gk_ref, wv_ref, cq_ref, sq_ref, ck_ref, sk_ref,
                     qh_ref, kh_ref, v_ref, rows_ref, *, tm):
    hb = _rms(x_ref[...], g_ref[...]).astype(BF16)
    z = _dot(hb, win_ref[...])
    cq = _rms(z[:, 0:Q_RANK], qag_ref[...])
    ckv = _rms(z[:, Q_RANK:Q_RANK + KV_RANK], kvg_ref[...])
    zr = z[:, Q_RANK + KV_RANK:Q_RANK + KV_RANK + 128]
    kr = zr * lax.rsqrt(jnp.sum(zr * zr, axis=-1, keepdims=True) * (1.0 / ROPE_C) + EPS) * rkg_ref[...]
    lane = lax.broadcasted_iota(jnp.int32, (tm, 128), 1)
    half = ROPE_C // 2
    rot_k = jnp.where(lane < half, -pltpu.roll(kr, 128 - half, 1),
                      jnp.where(lane < ROPE_C, pltpu.roll(kr, half, 1), 0.0))
    krf = kr * ck_ref[...] + rot_k * sk_ref[...]
    rows_ref[:, 0:KV_RANK] = ckv
    rows_ref[:, KV_RANK:LAT_C] = krf[:, 0:ROPE_C]

    cqb = cq.astype(BF16)
    ckvb = ckv.astype(BF16)
    gm = gm_ref[...]
    kr_place = pltpu.roll(krf, NOPE_C, 1)
    kr2 = jnp.concatenate([kr_place, kr_place], axis=1)
    cos2 = jnp.concatenate([cq_ref[...], cq_ref[...]], axis=1)
    sin2 = jnp.concatenate([sq_ref[...], sq_ref[...]], axis=1)
    lane2 = lax.broadcasted_iota(jnp.int32, (tm, 256), 1) % 128
    first_half = jnp.logical_and(lane2 >= NOPE_C, lane2 < NOPE_C + half)
    second_half = jnp.logical_and(lane2 >= NOPE_C + half, lane2 < NOPE_C + ROPE_C)
    for c in range(H_C * HEAD_PAD // 256):
        cols = slice(c * 256, (c + 1) * 256)
        zq = _dot(cqb, wq_ref[:, cols])
        qn = zq * lax.rsqrt(_dot((zq * zq).astype(BF16), gm) + EPS) * gq_ref[:, cols]
        rot_q = jnp.where(first_half, -pltpu.roll(qn, 256 - half, 1),
                          jnp.where(second_half, pltpu.roll(qn, half, 1), 0.0))
        qh_ref[:, cols] = (qn * cos2 + rot_q * sin2).astype(BF16)
        zk = _dot(ckvb, wk_ref[:, cols])
        kn = zk * lax.rsqrt(_dot((zk * zk).astype(BF16), gm) + EPS) * gk_ref[:, cols] + kr2
        kh_ref[:, cols] = kn.astype(BF16)
    v_ref[...] = _dot(ckvb, wv_ref[...]).astype(BF16)


def _mla_proj(x, g, win, qag, kvg, rkg, wq, gq, gm, wk, gk, wv, tabs, *, tab_tiles, tm):
    n, d = x.shape
    tm = _row_tile(n, tm)
    hw = H_C * HEAD_PAD
    row = lambda c: pl.BlockSpec((tm, c), lambda i: (i, 0))
    tab = pl.BlockSpec((tm, 128), lambda i: (i % tab_tiles, 0))
    return pl.pallas_call(
        functools.partial(_mla_proj_kernel, tm=tm),
        grid=(n // tm,),
        in_specs=[row(d), _const((1, d)), _const((d, 512)), _const((1, Q_RANK)), _const((1, KV_RANK)),
                  _const((1, 128)), _const((Q_RANK, hw)), _const((1, hw)), _const((256, 256)),
                  _const((KV_RANK, hw)), _const((1, hw)), _const((KV_RANK, H_C * V_C)),
                  tab, tab, tab, tab],
        out_specs=[row(hw), row(hw), row(H_C * V_C), row(LAT_C)],
        out_shape=[jax.ShapeDtypeStruct((n, hw), BF16), jax.ShapeDtypeStruct((n, hw), BF16),
                   jax.ShapeDtypeStruct((n, H_C * V_C), BF16), jax.ShapeDtypeStruct((n, LAT_C), F32)],
        compiler_params=_params("parallel"),
        name="mla_proj",
    )(x, g, win, qag, kvg, rkg, wq, gq, gm, wk, gk, wv, *tabs)


def _diff_sample_kernel(pt_ref, qbd_ref, knew_ref, lam_ref, sg_ref, *refs, n_pages_step, n_steps, lam_init):
    del pt_ref
    page_refs = refs[:n_pages_step]
    o_ref, m_s, l_s, acc_s = refs[n_pages_step:]
    c = pl.program_id(1)

    @pl.when(c == 0)
    def _():
        m_s[...] = jnp.full_like(m_s, NEG_BIG)
        l_s[...] = jnp.zeros_like(l_s)
        acc_s[...] = jnp.zeros_like(acc_s)

    def gather(pr, first_row):
        return jnp.concatenate([pr[pl.ds(first_row + h, PAGE, stride=8), :] for h in range(H_A)],
                               axis=1).astype(BF16)

    qf = qbd_ref[0]
    qb = qf.astype(BF16)
    s = jnp.concatenate([_nt_dot(qb, gather(pr, 0)) for pr in page_refs], axis=1)
    m_old = m_s[...]
    m_new = jnp.maximum(m_old, jnp.max(s, axis=-1, keepdims=True))
    alpha = jnp.exp(m_old - m_new)
    p = jnp.exp(s - m_new)
    l_s[...] = alpha * l_s[...] + jnp.sum(p, axis=-1, keepdims=True)
    pb = p.astype(BF16)
    pv = _dot(pb[:, 0:PAGE], gather(page_refs[0], H_A))
    for i in range(1, n_pages_step):
        pv = pv + _dot(pb[:, i * PAGE:(i + 1) * PAGE], gather(page_refs[i], H_A))
    acc_s[...] = alpha * acc_s[...] + pv
    m_s[...] = m_new

    @pl.when(c == n_steps - 1)
    def _():
        row = lax.broadcasted_iota(jnp.int32, (32, 1), 0)
        qidx = row % 4
        m0 = m_s[...]
        sj = []
        for j in range(4):
            kj = knew_ref[0, j:j + 1, 0:512]
            sj.append(jnp.where(j <= qidx, jnp.sum(qf * kj, axis=-1, keepdims=True), NEG_BIG))
        m1 = jnp.maximum(jnp.maximum(jnp.maximum(m0, sj[0]), jnp.maximum(sj[1], sj[2])), sj[3])
        al = jnp.exp(m0 - m1)
        l1 = al * l_s[...]
        acc = al * acc_s[...]
        for j in range(4):
            pj = jnp.exp(sj[j] - m1)
            l1 = l1 + pj
            acc = acc + pj * knew_ref[0, j:j + 1, 512:1024]
        accn = acc * (1.0 / l1)
        lv = lam_ref[...]
        lam = (jnp.exp(jnp.sum(lv[0:1] * lv[1:2], axis=-1, keepdims=True))
               - jnp.exp(jnp.sum(lv[2:3] * lv[3:4], axis=-1, keepdims=True)) + lam_init)
        o = accn[0:16] - lam * accn[16:32]
        lane_head = lax.broadcasted_iota(jnp.int32, (16, 512), 1) // 128
        row_head = lax.broadcasted_iota(jnp.int32, (16, 512), 0) // 4
        o = jnp.where(lane_head == row_head, o, 0.0)
        o = (o * lax.rsqrt(jnp.sum(o * o, axis=-1, keepdims=True) * (1.0 / 128.0) + EPS)
             * sg_ref[...] * (1.0 - lam_init))
        o_ref[0] = o[0:4] + o[4:8] + o[8:12] + o[12:16]


def _diff_sample(page_table, qbd, knew, lam, sg, cache, *, n_pages_step, lam_init):
    db, n_pages = page_table.shape
    n_steps = n_pages // n_pages_step
    page_specs = [pl.BlockSpec((PAGE * 8, 128), lambda s, c, pt, i=i: (pt[s, c * n_pages_step + i], 0))
                  for i in range(n_pages_step)]
    grid_spec = pltpu.PrefetchScalarGridSpec(
        num_scalar_prefetch=1,
        grid=(db, n_steps),
        in_specs=[pl.BlockSpec((1, 32, 512), lambda s, c, pt: (s, 0, 0)),
                  pl.BlockSpec((1, 8, 1024), lambda s, c, pt: (s, 0, 0)),
                  pl.BlockSpec((4, HD_A), lambda s, c, pt: (0, 0)),
                  pl.BlockSpec((1, 512), lambda s, c, pt: (0, 0))] + page_specs,
        out_specs=pl.BlockSpec((1, 4, 512), lambda s, c, pt: (s, 0, 0)),
        scratch_shapes=[pltpu.VMEM((32, 1), F32), pltpu.VMEM((32, 1), F32), pltpu.VMEM((32, 512), F32)],
    )
    return pl.pallas_call(
        functools.partial(_diff_sample_kernel, n_pages_step=n_pages_step, n_steps=n_steps, lam_init=lam_init),
        grid_spec=grid_spec,
        out_shape=jax.ShapeDtypeStruct((db, 4, 512), F32),
        compiler_params=_params("arbitrary", "arbitrary"),
        name="diff_sample",
    )(page_table, qbd, knew, lam, sg, *([cache] * n_pages_step))


def _mla_sample_kernel(pt_ref, qbd_ref, qrope_ref, qrows_ref, knew_ref, cnew_ref, wkt_ref, wkg_ref, wv_ref,
                       *refs, n_pages_step, n_steps):
    del pt_ref
    page_refs = refs[:n_pages_step]
    o_ref, m_s, l_s, lat_s, qabs_s = refs[n_pages_step:]
    c = pl.program_id(1)
    nk = n_pages_step * PAGE

    @pl.when(c == 0)
    def _():
        m_s[...] = jnp.full_like(m_s, NEG_BIG)
        l_s[...] = jnp.zeros_like(l_s)
        lat_s[...] = jnp.zeros_like(lat_s)
        qabs_s[...] = _dot(qbd_ref[0], wkg_ref[...]).astype(BF16)

    ct = jnp.concatenate([pr[0:KV_RANK, :] for pr in page_refs], axis=1).astype(BF16)
    krt = jnp.concatenate([pr[KV_RANK:LAT_C, :] for pr in page_refs], axis=1)
    krt = jnp.concatenate([krt, jnp.zeros((128 - ROPE_C, nk), F32)], axis=0).astype(BF16)
    kpt = _dot(wkt_ref[...], ct)
    msq = jnp.sum((kpt * kpt).reshape(H_C, NOPE_C, nk), axis=1) * (1.0 / NOPE_C)
    rinv = lax.rsqrt(msq + EPS)
    s = (_dot(qabs_s[...], ct) * jnp.concatenate([rinv] * 4, axis=0)
         + _dot(qrope_ref[0], krt))
    m_old = m_s[...]
    m_new = jnp.maximum(m_old, jnp.max(s, axis=-1, keepdims=True))
    alpha = jnp.exp(m_old - m_new)
    p = jnp.exp(s - m_new)
    l_s[...] = alpha * l_s[...] + jnp.sum(p, axis=-1, keepdims=True)
    lat_s[...] = alpha * lat_s[...] + _nt_dot(p.astype(BF16), ct)
    m_s[...] = m_new

    @pl.when(c == n_steps - 1)
    def _():
        qrows = qrows_ref[0]
        qidx = lax.broadcasted_iota(jnp.int32, (64, 1), 0) // H_C
        m0 = m_s[...]
        sj = []
        for j in range(4):
            kt = jnp.concatenate([knew_ref[0, j]] * 4, axis=0)
            sj.append(jnp.where(j <= qidx, jnp.sum(qrows * kt, axis=-1, keepdims=True), NEG_BIG))
        m1 = jnp.maximum(jnp.maximum(jnp.maximum(m0, sj[0]), jnp.maximum(sj[1], sj[2])), sj[3])
        al = jnp.exp(m0 - m1)
        l1 = al * l_s[...]
        lat = al * lat_s[...]
        for j in range(4):
            pj = jnp.exp(sj[j] - m1)
            l1 = l1 + pj
            lat = lat + pj * cnew_ref[0, j:j + 1, :]
        latn = (lat * (1.0 / l1)).astype(BF16)
        full = _dot(latn, wv_ref[...])
        lane_head = lax.broadcasted_iota(jnp.int32, (64, H_C * V_C), 1) // V_C
        row_head = lax.broadcasted_iota(jnp.int32, (64, H_C * V_C), 0) % H_C
        full = jnp.where(lane_head == row_head, full, 0.0)
        for q in range(4):
            o_ref[0, q:q + 1, :] = jnp.sum(full[q * H_C:(q + 1) * H_C], axis=0, keepdims=True)
        o_ref[0, 4:8, :] = jnp.zeros((4, H_C * V_C), F32)


def _mla_sample(page_table, qbd, qrope, qrows, knew, cnew, wkt, wkg, wv, cache, *, n_pages_step):
    db, n_pages = page_table.shape
    n_steps = n_pages // n_pages_step
    seq3 = lambda a, b: pl.BlockSpec((1, a, b), lambda s, c, pt: (s, 0, 0))
    cst = lambda a, b: pl.BlockSpec((a, b), lambda s, c, pt: (0, 0))
    page_specs = [pl.BlockSpec((LAT_C, PAGE), lambda s, c, pt, i=i: (pt[s, c * n_pages_step + i], 0))
                  for i in range(n_pages_step)]
    grid_spec = pltpu.PrefetchScalarGridSpec(
        num_scalar_prefetch=1,
        grid=(db, n_steps),
        in_specs=[seq3(64, H_C * NOPE_C), seq3(64, 128), seq3(64, 128),
                  pl.BlockSpec((1, 4, H_C, 128), lambda s, c, pt: (s, 0, 0, 0)),
                  seq3(8, KV_RANK), cst(H_C * NOPE_C, KV_RANK), cst(H_C * NOPE_C, KV_RANK),
                  cst(KV_RANK, H_C * V_C)] + page_specs,
        out_specs=seq3(8, H_C * V_C),
        scratch_shapes=[pltpu.VMEM((64, 1), F32), pltpu.VMEM((64, 1), F32), pltpu.VMEM((64, KV_RANK), F32),
                        pltpu.VMEM((64, KV_RANK), BF16)],
    )
    return pl.pallas_call(
        functools.partial(_mla_sample_kernel, n_pages_step=n_pages_step, n_steps=n_steps),
        grid_spec=grid_spec,
        out_shape=jax.ShapeDtypeStruct((db, 8, H_C * V_C), F32),
        compiler_params=_params("arbitrary", "arbitrary"),
        name="mla_sample",
    )(page_table, qbd, qrope, qrows, knew, cnew, wkt, wkg, wv, *([cache] * n_pages_step))


def _group_mean_matrix(sizes, total):
    m = jnp.zeros((total, total), F32)
    off = 0
    for sz, live in sizes:
        if live:
            m = m.at[off:off + sz, off:off + sz].set(1.0 / sz)
        off += sz
    return m.astype(BF16)


def _rope_tables(pos, lane_offset):
    half = ROPE_C // 2
    inv = ROPE_THETA ** (-jnp.arange(half, dtype=F32) / half)
    ang = pos.astype(F32)[:, None] * inv[None, :]
    c2 = jnp.concatenate([jnp.cos(ang), jnp.cos(ang)], axis=1)
    s2 = jnp.concatenate([jnp.sin(ang), jnp.sin(ang)], axis=1)
    cos = jnp.ones((pos.shape[0], 128), F32).at[:, lane_offset:lane_offset + ROPE_C].set(c2)
    sin = jnp.zeros((pos.shape[0], 128), F32).at[:, lane_offset:lane_offset + ROPE_C].set(s2)
    return cos, sin


def kernel(x_prompt, x_sample, cache_kv_diff, cache_mla, state_lru_h, state_lru_conv, state_ffn_conv, page_table,
           norm_mix, norm_ffn, w_in_e, qk_norm_a, lambda_a, subln_a, conv_w_lru, conv_b_lru, gate_w_lru,
           gate_b_lru, lambda_lru, w_out_e, w_in_c, q_a_norm, kv_a_norm, w_uq, w_ukv, qn_c, rn_c, w_out_c,
           w_up, conv_w_ffn, conv_b_ffn, w_down):
    B, T, D = x_prompt.shape
    DB, S, _ = x_sample.shape
    n_pages = page_table.shape[1]
    n_past = n_pages * PAGE
    n_pool = cache_kv_diff.shape[1]
    depth = norm_mix.shape[0]
    NP, NS = B * T, S * DB
    pt = page_table.astype(jnp.int32)

    xp = x_prompt.reshape(NP, D)
    xs = jnp.swapaxes(x_sample, 0, 1).reshape(NS, D)

    def to_seq_major(a):
        return jnp.swapaxes(a.reshape((S, DB) + a.shape[1:]), 0, 1)

    pos_p = jnp.arange(T, dtype=jnp.int32)
    pos_s = jnp.repeat(n_past + jnp.arange(S, dtype=jnp.int32), DB)
    tabs_p = _rope_tables(pos_p, NOPE_C) + _rope_tables(pos_p, 0)
    tabs_s = _rope_tables(pos_s, NOPE_C) + _rope_tables(pos_s, 0)

    kv_p, kv_s, mla_p, mla_s = [], [], [], []
    lh_p, lh_s, lc_p, lc_s, fc_p, fc_s = [], [], [], [], [], []

    for l in range(depth):
        g_mix = norm_mix[l].reshape(1, D)
        if l % 2 == 0:
            e = l // 2
            lam_init = 0.8 - 0.6 * math.exp(-0.3 * l)
            w_in = w_in_e[e].astype(BF16)
            qs = jnp.tile(qk_norm_a[e, 0].reshape(1, 2 * HD_A), (1, H_A)) * (1.0 / math.sqrt(HD_A))
            ks = jnp.tile(qk_norm_a[e, 1].reshape(1, 2 * HD_A), (1, H_A))
            gm64 = _group_mean_matrix([(HD_A, True)] * (2 * H_A), 2 * H_A * HD_A)
            lam_v = lambda_a[e]
            sg = subln_a[e].reshape(1, 2 * HD_A)
            cw, cb = conv_w_lru[e], conv_b_lru[e].reshape(1, D_RNN)
            gw = jnp.zeros((2, 8, 64, 8, 64), F32)
            gw = gw.at[:, jnp.arange(8), :, jnp.arange(8), :].set(jnp.swapaxes(gate_w_lru[e], 0, 1))
            gw = jnp.concatenate([gw[0].reshape(D_RNN, D_RNN), gw[1].reshape(D_RNN, D_RNN)], axis=1).astype(BF16)
            gb = gate_b_lru[e].reshape(1, 2 * D_RNN)
            lam_lru = lambda_lru[e].reshape(1, D_RNN)
            w_out = w_out_e[e].astype(BF16)

            qb_p, kb_p, vb_p, kvf_p, xg_p = _even_proj(xp, g_mix, w_in, qs, ks, gm64, 512)
            qb_s, kb_s, vb_s, kvf_s, xg_s = _even_proj(xs, g_mix, w_in, qs, ks, gm64, 512)

            a_p = _flash(qb_p, kb_p, vb_p, (lam_v, sg), diff=True, batch=B, seq=T, tb=512, npair=2, lam_init=lam_init)

            q_sm = to_seq_major(qb_s.astype(F32)).reshape(DB, S, H_A, 2, HD_A)
            eye_h = jnp.eye(H_A, dtype=F32)
            eye_m = jnp.eye(2, dtype=F32)
            qbd = jnp.einsum('bqhmd,hg,mn->bmhqgnd', q_sm, eye_h, eye_m)
            qbd = qbd.reshape(DB, 2 * H_A * S, H_A * 2 * HD_A)
            knew = jnp.pad(to_seq_major(kvf_s), ((0, 0), (0, 8 - S), (0, 0)))
            cache_d = cache_kv_diff.reshape(cache_kv_diff.shape[0] * n_pool * PAGE * 2 * H_A, 2 * HD_A)
            sg4 = jnp.tile(sg, (1, H_A))
            a_s = _diff_sample(pt + e * n_pool, qbd, knew, lam_v, sg4, cache_d,
                               n_pages_step=min(16, n_pages), lam_init=lam_init)
            a_s = jnp.swapaxes(a_s, 0, 1).reshape(NS, H_A * 2 * HD_A).astype(BF16)

            y_p, hl_pp, ct_pp = _lru_prompt(xg_p, cw, cb, gw, gb, lam_lru, batch=B, seq=T, tm=256)
            hist = jnp.swapaxes(state_lru_conv[e], 0, 1).reshape(3 * DB, D_RNN)
            y_s, hl_ss, ct_ss = _lru_sample(xg_s, hist, state_lru_h[e], cw, cb, gw, gb, lam_lru, db=DB, steps=S)

            xp = _mm_res(xp, [a_p, y_p], w_out, 512)
            xs = _mm_res(xs, [a_s, y_s], w_out, 512)

            kv_p.append(kvf_p.reshape(B, T, 2, H_A, 2 * HD_A))
            kv_s.append(to_seq_major(kvf_s).reshape(DB, S, 2, H_A, 2 * HD_A))
            nt = hl_pp.shape[0] // B
            lh_p.append(hl_pp.reshape(B, nt, 8, D_RNN)[:, -1, -1])
            lc_p.append(ct_pp.reshape(B, nt, 8, D_RNN)[:, -1, 5:8])
            lh_s.append(hl_ss)
            lc_s.append(jnp.swapaxes(ct_ss.reshape(3, DB, D_RNN), 0, 1))
        else:
            o = l // 2
            scale = 1.0 / math.sqrt(NOPE_C + ROPE_C)
            pad_h = HEAD_PAD - NOPE_C - ROPE_C
            win = jnp.pad(w_in_c[o], ((0, 0), (0, 512 - w_in_c.shape[2]))).astype(BF16)
            qag = q_a_norm[o].reshape(1, Q_RANK)
            kvg = kv_a_norm[o].reshape(1, KV_RANK)
            rkg = jnp.pad(rn_c[o, 1], (0, 128 - ROPE_C)).reshape(1, 128)
            wq = jnp.pad(w_uq[o], ((0, 0), (0, 0), (0, pad_h))).reshape(Q_RANK, H_C * HEAD_PAD).astype(BF16)
            gq = jnp.tile(jnp.concatenate([qn_c[o, 0], rn_c[o, 0], jnp.zeros((pad_h,), F32)]) * scale,
                          H_C).reshape(1, H_C * HEAD_PAD)
            gm256 = _group_mean_matrix([(NOPE_C, True), (ROPE_C, True), (pad_h, False)] * 2, 256)
            wk_nope = w_ukv[o][:, :, :NOPE_C]
            wk = jnp.pad(wk_nope, ((0, 0), (0, 0), (0, HEAD_PAD - NOPE_C))).reshape(KV_RANK, H_C * HEAD_PAD)
            wk = wk.astype(BF16)
            gk = jnp.tile(jnp.concatenate([qn_c[o, 1], jnp.zeros((HEAD_PAD - NOPE_C,), F32)]),
                          H_C).reshape(1, H_C * HEAD_PAD)
            wv = w_ukv[o][:, :, NOPE_C:].reshape(KV_RANK, H_C * V_C).astype(BF16)
            wkt = jnp.transpose(wk_nope.reshape(KV_RANK, H_C * NOPE_C))
            wkg = (wkt * jnp.tile(qn_c[o, 1], H_C)[:, None]).astype(BF16)
            wkt = wkt.astype(BF16)
            w_out = w_out_c[o].astype(BF16)

            tm_p = _row_tile(T, 512)
            qh_p, kh_p, v_p, rows_p = _mla_proj(xp, g_mix, win, qag, kvg, rkg, wq, gq, gm256, wk, gk, wv, tabs_p,
                                                tab_tiles=T // tm_p, tm=tm_p)
            qh_s, kh_s, v_s, rows_s = _mla_proj(xs, g_mix, win, qag, kvg, rkg, wq, gq, gm256, wk, gk, wv, tabs_s,
                                                tab_tiles=1, tm=NS)

            o_p = _flash(qh_p, kh_p, v_p, (), diff=False, batch=B, seq=T, tb=512, npair=2)

            qh_sm = to_seq_major(qh_s.astype(F32)).reshape(DB, S, H_C, HEAD_PAD)
            qrows = qh_sm.reshape(DB, S * H_C, HEAD_PAD)
            qn = qh_sm[..., :NOPE_C]
            qbd = jnp.einsum('bqhd,hg->bqhgd', qn, jnp.eye(H_C, dtype=F32)).reshape(DB, S * H_C, H_C * NOPE_C)
            qrope = jnp.pad(qh_sm[..., NOPE_C:NOPE_C + ROPE_C],
                            ((0, 0), (0, 0), (0, 0), (0, 128 - ROPE_C))).reshape(DB, S * H_C, 128)
            knew = to_seq_major(kh_s.astype(F32)).reshape(DB, S, H_C, HEAD_PAD)
            rows_sm = to_seq_major(rows_s)
            cnew = jnp.pad(rows_sm[..., :KV_RANK], ((0, 0), (0, 8 - S), (0, 0)))
            cache_m = jnp.swapaxes(cache_mla, 2, 3).reshape(cache_mla.shape[0] * n_pool * LAT_C, PAGE)
            o_s = _mla_sample(pt + o * n_pool, qbd.astype(BF16), qrope.astype(BF16), qrows, knew, cnew,
                              wkt, wkg, wv, cache_m, n_pages_step=min(16, n_pages))
            o_s = jnp.swapaxes(o_s[:, :S], 0, 1).reshape(NS, H_C * V_C).astype(BF16)

            xp = _mm_res(xp, [o_p], w_out, 512)
            xs = _mm_res(xs, [o_s], w_out, 512)
            mla_p.append(rows_p.reshape(B, T, LAT_C))
            mla_s.append(rows_sm)

        g_ffn = norm_ffn[l].reshape(1, D)
        wu = w_up[l].astype(BF16)
        wd = w_down[l].astype(BF16)
        cwf, cbf = conv_w_ffn[l], conv_b_ffn[l].reshape(1, 2 * D_FF)
        xp, tg, tl = _ffn_prompt(xp, g_ffn, wu, cwf, cbf, wd, seq=T, tm=1024, tf=1408)
        ntf = tg.shape[0] // B
        tail = jnp.concatenate([tg, tl], axis=-1).reshape(B, ntf, 8, 2 * D_FF)
        fc_p.append(tail[:, -1, 6:8])
        hist_f = jnp.swapaxes(state_ffn_conv[l], 0, 1).reshape(2 * DB, 2 * D_FF)
        xs, tgs, tls = _ffn_sample(xs, g_ffn, wu, cwf, cbf, wd, hist_f, db=DB, steps=S, tf=1408)
        fc_s.append(jnp.swapaxes(jnp.concatenate([tgs, tls], axis=-1).reshape(2, DB, 2 * D_FF), 0, 1))

    y_p = xp.reshape(B, T, D)
    y_s = to_seq_major(xs)
    return (y_p, y_s, jnp.stack(kv_p), jnp.stack(kv_s), jnp.stack(mla_p), jnp.stack(mla_s),
            jnp.stack(lh_p), jnp.stack(lh_s), jnp.stack(lc_p), jnp.stack(lc_s), jnp.stack(fc_p), jnp.stack(fc_s))
```

```python
import functools
import math

import jax
import jax.numpy as jnp
from jax import lax
from jax.experimental import pallas as pl
from jax.experimental.pallas import tpu as pltpu

F32 = jnp.float32
BF16 = jnp.bfloat16

EPS = 1e-6
NEG_BIG = -1e30
ROPE_THETA = 10000.0
LRU_C = 8.0
PAGE = 128

H_A = 4
HD_A = 64
D_RNN = 512
H_C = 16
Q_RANK = 256
KV_RANK = 128
NOPE_C = 64
ROPE_C = 32
V_C = 64
LAT_C = KV_RANK + ROPE_C
HEAD_PAD = 128
D_FF = 2816

VMEM_LIMIT_BYTES = 56 * 1024 * 1024


def _params(*semantics):
    return pltpu.CompilerParams(dimension_semantics=semantics,
                                vmem_limit_bytes=VMEM_LIMIT_BYTES)


def _row_tile(n, want):
    t = min(n, want)
    assert n % t == 0
    return t


def _const(shape):
    return pl.BlockSpec(shape, lambda *_: (0,) * len(shape))


def _rms(x, g):
    return x * lax.rsqrt(jnp.mean(x * x, axis=-1, keepdims=True) + EPS) * g


def _nt_dot(a, b):
    return lax.dot_general(a, b, (((1,), (1,)), ((), ())), preferred_element_type=F32)


def _dot(a, b):
    return jnp.dot(a, b, preferred_element_type=F32)


def _gelu_tanh(x):
    return 0.5 * x * (1.0 + jnp.tanh(math.sqrt(2.0 / math.pi) * (x + 0.044715 * x * x * x)))


def _softplus(y):
    return jnp.maximum(y, 0.0) + jnp.log1p(jnp.exp(-jnp.abs(y)))


def _shift_rows(x, prev8, s):
    xr = pltpu.roll(x, s, 0)
    row8 = lax.broadcasted_iota(jnp.int32, prev8.shape, 0)
    top = jnp.where(row8 < s, pltpu.roll(prev8, s, 0), xr[0:8])
    return jnp.concatenate([top, xr[8:]], axis=0)


def _even_proj_kernel(x_ref, g_ref, w_ref, qs_ref, ks_ref, gm_ref,
                      qb_ref, kb_ref, vb_ref, kv_ref, xg_ref):
    hb = _rms(x_ref[...], g_ref[...]).astype(BF16)
    z = _dot(hb, w_ref[...])
    zq, zk, zv = z[:, 0:512], z[:, 512:1024], z[:, 1024:1536]
    gm = gm_ref[...]
    mq = _dot((zq * zq).astype(BF16), gm)
    mk = _dot((zk * zk).astype(BF16), gm)
    q = zq * lax.rsqrt(mq + EPS) * qs_ref[...]
    k = zk * lax.rsqrt(mk + EPS) * ks_ref[...]
    qb_ref[...] = q.astype(BF16)
    kb_ref[...] = k.astype(BF16)
    vb_ref[...] = zv.astype(BF16)
    kv_ref[:, 0:512] = k
    kv_ref[:, 512:1024] = zv
    xg_ref[...] = z[:, 1536:2560]


def _even_proj(x, g, w, qs, ks, gm, tm):
    n = x.shape[0]
    tm = _row_tile(n, tm)
    row = lambda c: pl.BlockSpec((tm, c), lambda i: (i, 0))
    return pl.pallas_call(
        _even_proj_kernel,
        grid=(n // tm,),
        in_specs=[row(1024), _const((1, 1024)), _const((1024, 2560)), _const((1, 512)),
                  _const((1, 512)), _const((512, 512))],
        out_specs=[row(512), row(512), row(512), row(1024), row(1024)],
        out_shape=[jax.ShapeDtypeStruct((n, 512), BF16), jax.ShapeDtypeStruct((n, 512), BF16),
                   jax.ShapeDtypeStruct((n, 512), BF16), jax.ShapeDtypeStruct((n, 1024), F32),
                   jax.ShapeDtypeStruct((n, 1024), F32)],
        compiler_params=_params("parallel"),
        name="even_proj",
    )(x, g, w, qs, ks, gm)


def _online_update(s, m, acc, vext):
    m_new = jnp.maximum(m, jnp.max(s, axis=-1, keepdims=True))
    alpha = jnp.exp(m - m_new)
    p = jnp.exp(s - m_new).astype(BF16)
    return m_new, alpha * acc + _dot(p, vext)


def _flash_kernel(*refs, diff, tb, npair, lam_init):
    if diff:
        q_ref, k_ref, v_ref, lam_ref, sg_ref, o_ref = refs
    else:
        q_ref, k_ref, v_ref, o_ref = refs
    qi = pl.program_id(2)
    qw = 128 if diff else 256
    nchain = 2 * npair
    lane = lax.broadcasted_iota(jnp.int32, (tb, 128), 1)
    qs = []
    for g in range(npair):
        q = q_ref[:, g * qw:(g + 1) * qw]
        if diff:
            zero = jnp.zeros_like(q)
            qs.append(jnp.where(lane < HD_A, q, zero))
            qs.append(jnp.where(lane >= HD_A, q, zero))
        else:
            qs += [q[:, 0:128], q[:, 128:256]]

    ones = jnp.ones((tb, 128), BF16)

    def scores(j):
        start = pl.multiple_of(j * tb, tb)
        out = []
        for g in range(npair):
            kblk = k_ref[pl.ds(start, tb), g * qw:(g + 1) * qw]
            ks = [kblk, kblk] if diff else [kblk[:, 0:128], kblk[:, 128:256]]
            out += [_nt_dot(qs[2 * g + c], ks[c]) for c in range(2)]
        return out

    def values(j):
        start = pl.multiple_of(j * tb, tb)
        out = []
        for g in range(npair):
            v = v_ref[pl.ds(start, tb), g * 128:(g + 1) * 128]
            if diff:
                vext = jnp.concatenate([v, ones], axis=1)
                out += [vext, vext]
            else:
                out += [jnp.where(lane < V_C, v, ones), jnp.where(lane < V_C, ones, v)]
        return out

    def body(j, carry):
        s, vs = scores(j), values(j)
        out = []
        for c in range(nchain):
            out += list(_online_update(s[c], carry[2 * c], carry[2 * c + 1], vs[c]))
        return tuple(out)

    col1 = jnp.full((tb, 1), NEG_BIG, F32)
    zacc = jnp.zeros((tb, 256 if diff else 128), F32)
    carry = lax.fori_loop(0, qi, body, (col1, zacc) * nchain)

    s, vs = scores(qi), values(qi)
    r = lax.broadcasted_iota(jnp.int32, (tb, tb), 0)
    c = lax.broadcasted_iota(jnp.int32, (tb, tb), 1)
    keep = c <= r
    for g in range(npair):
        os_ = []
        for c2 in (2 * g, 2 * g + 1):
            _, a = _online_update(jnp.where(keep, s[c2], NEG_BIG), carry[2 * c2], carry[2 * c2 + 1], vs[c2])
            if diff:
                os_.append(a[:, 0:128] * (1.0 / a[:, 128:129]))
            else:
                den = a[:, 64:65] if c2 % 2 == 0 else a[:, 0:1]
                os_.append(a * (1.0 / den))
        if diff:
            lv = lam_ref[...]
            lam = (jnp.exp(jnp.sum(lv[0:1] * lv[1:2], axis=-1, keepdims=True))
                   - jnp.exp(jnp.sum(lv[2:3] * lv[3:4], axis=-1, keepdims=True)) + lam_init)
            o = os_[0] - lam * os_[1]
            o = _rms(o, sg_ref[...]) * (1.0 - lam_init)
        else:
            o = jnp.where(lane < V_C, os_[0], os_[1])
        o_ref[:, g * 128:(g + 1) * 128] = o.astype(o_ref.dtype)


def _flash(q, k, v, extra, *, diff, batch, seq, tb, npair, lam_init=0.0):
    n = q.shape[0]
    tb = _row_tile(seq, tb)
    nq = seq // tb
    qw = (128 if diff else 256) * npair
    groups = q.shape[1] // qw
    in_specs = [pl.BlockSpec((tb, qw), lambda b, g, i: (b * nq + i, g)),
                pl.BlockSpec((seq, qw), lambda b, g, i: (b, g)),
                pl.BlockSpec((seq, 128 * npair), lambda b, g, i: (b, g))]
    in_specs += [_const(e.shape) for e in extra]
    return pl.pallas_call(
        functools.partial(_flash_kernel, diff=diff, tb=tb, npair=npair, lam_init=lam_init),
        grid=(batch, groups, nq),
        in_specs=in_specs,
        out_specs=pl.BlockSpec((tb, 128 * npair), lambda b, g, i: (b * nq + i, g)),
        out_shape=jax.ShapeDtypeStruct((n, groups * 128 * npair), BF16),
        compiler_params=_params("parallel", "parallel", "arbitrary"),
        name="flash_diff" if diff else "flash_mla",
    )(q, k, v, *extra)


def _lru_gates(xc, gw_ref, gb_ref, lam_ref):
    g = _dot(xc.astype(BF16), gw_ref[...]) + gb_ref[...]
    r = jax.nn.sigmoid(g[:, 0:D_RNN])
    i = jax.nn.sigmoid(g[:, D_RNN:2 * D_RNN])
    a = jnp.exp(-LRU_C * r * _softplus(-lam_ref[...]))
    mult = jnp.sqrt(1.0 - a * a)
    return a, mult, i


def _lru_prompt_kernel(x_ref, gr_ref, cw_ref, cb_ref, gw_ref, gb_ref, lam_ref,
                       y_ref, hl_ref, ct_ref, xprev_s, h_s, *, tm):
    ti = pl.program_id(1)

    @pl.when(ti == 0)
    def _():
        xprev_s[...] = jnp.zeros_like(xprev_s)
        h_s[...] = jnp.zeros_like(h_s)

    x = x_ref[...]
    prev8 = xprev_s[...]
    cw = cw_ref[...]
    xc = (cb_ref[...] + cw[0:1] * _shift_rows(x, prev8, 3) + cw[1:2] * _shift_rows(x, prev8, 2)
          + cw[2:3] * _shift_rows(x, prev8, 1) + cw[3:4] * x)
    a, mult, i = _lru_gates(xc, gw_ref, gb_ref, lam_ref)
    row = lax.broadcasted_iota(jnp.int32, (tm, D_RNN), 0)
    mult = jnp.where(jnp.logical_and(row == 0, ti == 0), 1.0, mult)
    b = mult * i * xc
    d = 1
    while d < tm:
        ar = jnp.where(row >= d, pltpu.roll(a, d, 0), 1.0)
        br = jnp.where(row >= d, pltpu.roll(b, d, 0), 0.0)
        b = a * br + b
        a = a * ar
        d *= 2
    h = a * h_s[7:8] + b
    y_ref[...] = (h * _gelu_tanh(gr_ref[...])).astype(BF16)
    h_s[...] = h[tm - 8:tm]
    hl_ref[0] = h[tm - 8:tm]
    ct_ref[0] = x[tm - 8:tm]
    xprev_s[...] = x[tm - 8:tm]


def _lru_prompt(xg, cw, cb, gw, gb, lam, *, batch, seq, tm):
    n = xg.shape[0]
    tm = _row_tile(seq, tm)
    nt = seq // tm
    return pl.pallas_call(
        functools.partial(_lru_prompt_kernel, tm=tm),
        grid=(batch, nt),
        in_specs=[pl.BlockSpec((tm, D_RNN), lambda b, t: (b * nt + t, 0)),
                  pl.BlockSpec((tm, D_RNN), lambda b, t: (b * nt + t, 1)),
                  _const((4, D_RNN)), _const((1, D_RNN)), _const((D_RNN, 2 * D_RNN)),
                  _const((1, 2 * D_RNN)), _const((1, D_RNN))],
        out_specs=[pl.BlockSpec((tm, D_RNN), lambda b, t: (b * nt + t, 0)),
                   pl.BlockSpec((1, 8, D_RNN), lambda b, t: (b * nt + t, 0, 0)),
                   pl.BlockSpec((1, 8, D_RNN), lambda b, t: (b * nt + t, 0, 0))],
        out_shape=[jax.ShapeDtypeStruct((n, D_RNN), BF16),
                   jax.ShapeDtypeStruct((batch * nt, 8, D_RNN), F32),
                   jax.ShapeDtypeStruct((batch * nt, 8, D_RNN), F32)],
        scratch_shapes=[pltpu.VMEM((8, D_RNN), F32), pltpu.VMEM((8, D_RNN), F32)],
        compiler_params=_params("arbitrary", "arbitrary"),
        name="lru_prompt",
    )(xg, xg, cw, cb, gw, gb, lam)


def _lru_sample_kernel(x_ref, gr_ref, hist_ref, h0_ref, cw_ref, cb_ref, gw_ref, gb_ref, lam_ref,
                       y_ref, hl_ref, ct_ref, *, db, steps):
    n = steps * db
    ext = jnp.concatenate([hist_ref[...], x_ref[...]], axis=0)
    cw = cw_ref[...]
    xc = (cb_ref[...] + cw[0:1] * ext[0:n] + cw[1:2] * ext[db:db + n]
          + cw[2:3] * ext[2 * db:2 * db + n] + cw[3:4] * ext[3 * db:3 * db + n])
    a, mult, i = _lru_gates(xc, gw_ref, gb_ref, lam_ref)
    b = mult * i * xc
    h = h0_ref[...]
    hs = []
    for t in range(steps):
        h = a[t * db:(t + 1) * db] * h + b[t * db:(t + 1) * db]
        hs.append(h)
    hseq = jnp.concatenate(hs, axis=0)
    y_ref[...] = (hseq * _gelu_tanh(gr_ref[...])).astype(BF16)
    hl_ref[...] = h
    ct_ref[...] = ext[n:n + 3 * db]


def _lru_sample(xg, hist, h0, cw, cb, gw, gb, lam, *, db, steps):
    n = steps * db
    return pl.pallas_call(
        functools.partial(_lru_sample_kernel, db=db, steps=steps),
        grid=(1,),
        in_specs=[pl.BlockSpec((n, D_RNN), lambda i: (0, 0)),
                  pl.BlockSpec((n, D_RNN), lambda i: (0, 1)),
                  _const((3 * db, D_RNN)), _const((db, D_RNN)),
                  _const((4, D_RNN)), _const((1, D_RNN)), _const((D_RNN, 2 * D_RNN)),
                  _const((1, 2 * D_RNN)), _const((1, D_RNN))],
        out_specs=[_const((n, D_RNN)), _const((db, D_RNN)), _const((3 * db, D_RNN))],
        out_shape=[jax.ShapeDtypeStruct((n, D_RNN), BF16),
                   jax.ShapeDtypeStruct((db, D_RNN), F32),
                   jax.ShapeDtypeStruct((3 * db, D_RNN), F32)],
        compiler_params=_params("arbitrary"),
        name="lru_sample",
    )(xg, xg, hist, h0, cw, cb, gw, gb, lam)


def _mm_res_kernel(*refs, n_in):
    res_ref, o_ref = refs[0], refs[-1]
    acc = res_ref[...]
    for a_ref, w_ref in zip(refs[1:1 + n_in], refs[1 + n_in:1 + 2 * n_in]):
        acc = acc + _dot(a_ref[...], w_ref[...])
    o_ref[...] = acc


def _mm_res(res, acts, w, tm):
    n, d = res.shape
    tm = _row_tile(n, tm)
    n_in = len(acts)
    kw = w.shape[0] // n_in
    in_specs = [pl.BlockSpec((tm, d), lambda i: (i, 0))]
    in_specs += [pl.BlockSpec((tm, kw), lambda i: (i, 0)) for _ in acts]
    in_specs += [pl.BlockSpec((kw, d), lambda i, j=j: (j, 0)) for j in range(n_in)]
    return pl.pallas_call(
        functools.partial(_mm_res_kernel, n_in=n_in),
        grid=(n // tm,),
        in_specs=in_specs,
        out_specs=pl.BlockSpec((tm, d), lambda i: (i, 0)),
        out_shape=jax.ShapeDtypeStruct((n, d), F32),
        compiler_params=_params("parallel"),
        name="out_proj",
    )(res, *acts, *([w] * n_in))


def _ffn_prompt_kernel(x_ref, g_ref, wg_ref, wl_ref, cwg_ref, cwl_ref, cbg_ref, cbl_ref, wd_ref,
                       o_ref, tg_ref, tl_ref, hb_s, cg_s, cl_s, *, tm, tf, tiles_per_seq):
    i = pl.program_id(0)
    j = pl.program_id(1)

    @pl.when(j == 0)
    def _():
        hb_s[...] = _rms(x_ref[...], g_ref[...]).astype(BF16)
        o_ref[...] = x_ref[...]

    @pl.when((i % tiles_per_seq) == 0)
    def _():
        cg_s[j] = jnp.zeros((8, tf), F32)
        cl_s[j] = jnp.zeros((8, tf), F32)

    hb = hb_s[...]
    prev_g, prev_l = cg_s[j], cl_s[j]
    acts, tails_g, tails_l = [], [], []
    for c0 in range(0, tf, 256):
        cols = slice(c0, min(c0 + 256, tf))

        def conv(w_ref, cw_ref, cb_ref, prev):
            u = _dot(hb, w_ref[:, cols])
            cw = cw_ref[:, cols]
            p8 = prev[:, cols]
            c = cb_ref[:, cols] + cw[0:1] * _shift_rows(u, p8, 2) + cw[1:2] * _shift_rows(u, p8, 1) + cw[2:3] * u
            return c, u[tm - 8:tm]

        cg, tg = conv(wg_ref, cwg_ref, cbg_ref, prev_g)
        cl, tl = conv(wl_ref, cwl_ref, cbl_ref, prev_l)
        tails_g.append(tg)
        tails_l.append(tl)
        acts.append((cg * jax.nn.sigmoid(cg) * cl).astype(BF16))
    tail_g = jnp.concatenate(tails_g, axis=1)
    tail_l = jnp.concatenate(tails_l, axis=1)
    cg_s[j] = tail_g
    cl_s[j] = tail_l
    tg_ref[0] = tail_g
    tl_ref[0] = tail_l
    o_ref[...] += _dot(jnp.concatenate(acts, axis=1), wd_ref[...])


def _ffn_prompt(x, g, w_up, cw, cb, w_down, *, seq, tm, tf):
    n, d = x.shape
    tm = _row_tile(seq, tm)
    nf = D_FF // tf
    nt = n // tm
    return pl.pallas_call(
        functools.partial(_ffn_prompt_kernel, tm=tm, tf=tf, tiles_per_seq=seq // tm),
        grid=(nt, nf),
        in_specs=[pl.BlockSpec((tm, d), lambda i, j: (i, 0)), _const((1, d)),
                  pl.BlockSpec((d, tf), lambda i, j: (0, j)),
                  pl.BlockSpec((d, tf), lambda i, j: (0, j + nf)),
                  pl.BlockSpec((3, tf), lambda i, j: (0, j)),
                  pl.BlockSpec((3, tf), lambda i, j: (0, j + nf)),
                  pl.BlockSpec((1, tf), lambda i, j: (0, j)),
                  pl.BlockSpec((1, tf), lambda i, j: (0, j + nf)),
                  pl.BlockSpec((tf, d), lambda i, j: (j, 0))],
        out_specs=[pl.BlockSpec((tm, d), lambda i, j: (i, 0)),
                   pl.BlockSpec((1, 8, tf), lambda i, j: (i, 0, j)),
                   pl.BlockSpec((1, 8, tf), lambda i, j: (i, 0, j))],
        out_shape=[jax.ShapeDtypeStruct((n, d), F32),
                   jax.ShapeDtypeStruct((nt, 8, D_FF), F32),
                   jax.ShapeDtypeStruct((nt, 8, D_FF), F32)],
        scratch_shapes=[pltpu.VMEM((tm, d), BF16), pltpu.VMEM((nf, 8, tf), F32),
                        pltpu.VMEM((nf, 8, tf), F32)],
        compiler_params=_params("arbitrary", "arbitrary"),
        name="ffn_prompt",
    )(x, g, w_up, w_up, cw, cw, cb, cb, w_down)


def _ffn_sample_kernel(x_ref, g_ref, wg_ref, wl_ref, cwg_ref, cwl_ref, cbg_ref, cbl_ref, wd_ref,
                       hg_ref, hl_ref, o_ref, tg_ref, tl_ref, hb_s, *, db, steps):
    j = pl.program_id(0)
    n = steps * db

    @pl.when(j == 0)
    def _():
        hb_s[...] = _rms(x_ref[...], g_ref[...]).astype(BF16)

    hb = hb_s[...]

    def branch(w_ref, cw_ref, cb_ref, hist_ref, tail_ref):
        u = _dot(hb, w_ref[...])
        ext = jnp.concatenate([hist_ref[...], u], axis=0)
        cw = cw_ref[...]
        c = cb_ref[...] + cw[0:1] * ext[0:n] + cw[1:2] * ext[db:db + n] + cw[2:3] * ext[2 * db:2 * db + n]
        tail_ref[...] = ext[n:n + 2 * db]
        return c

    cg = branch(wg_ref, cwg_ref, cbg_ref, hg_ref, tg_ref)
    cl = branch(wl_ref, cwl_ref, cbl_ref, hl_ref, tl_ref)
    act = (cg * jax.nn.sigmoid(cg) * cl).astype(BF16)
    contrib = _dot(act, wd_ref[...])

    @pl.when(j == 0)
    def _():
        o_ref[...] = x_ref[...] + contrib

    @pl.when(j > 0)
    def _():
        o_ref[...] += contrib


def _ffn_sample(x, g, w_up, cw, cb, w_down, hist, *, db, steps, tf):
    n, d = x.shape
    nf = D_FF // tf
    return pl.pallas_call(
        functools.partial(_ffn_sample_kernel, db=db, steps=steps),
        grid=(nf,),
        in_specs=[_const((n, d)), _const((1, d)),
                  pl.BlockSpec((d, tf), lambda j: (0, j)),
                  pl.BlockSpec((d, tf), lambda j: (0, j + nf)),
                  pl.BlockSpec((3, tf), lambda j: (0, j)),
                  pl.BlockSpec((3, tf), lambda j: (0, j + nf)),
                  pl.BlockSpec((1, tf), lambda j: (0, j)),
                  pl.BlockSpec((1, tf), lambda j: (0, j + nf)),
                  pl.BlockSpec((tf, d), lambda j: (j, 0)),
                  pl.BlockSpec((2 * db, tf), lambda j: (0, j)),
                  pl.BlockSpec((2 * db, tf), lambda j: (0, j + nf))],
        out_specs=[_const((n, d)),
                   pl.BlockSpec((2 * db, tf), lambda j: (0, j)),
                   pl.BlockSpec((2 * db, tf), lambda j: (0, j))],
        out_shape=[jax.ShapeDtypeStruct((n, d), F32),
                   jax.ShapeDtypeStruct((2 * db, D_FF), F32),
                   jax.ShapeDtypeStruct((2 * db, D_FF), F32)],
        scratch_shapes=[pltpu.VMEM((n, d), BF16)],
        compiler_params=_params("arbitrary"),
        name="ffn_sample",
    )(x, g, w_up, w_up, cw, cw, cb, cb, w_down, hist, hist)


def _mla_proj_kernel(x_ref, g_ref, win_ref, qag_ref, kvg_ref, rkg_ref, wq_ref, gq_ref, gm_ref,
                     wk_ref, gk_ref, wv_ref, cq_ref, sq_ref, ck_ref, sk_ref,
                     qh_ref, kh_ref, v_ref, rows_ref, *, tm):
    hb = _rms(x_ref[...], g_ref[...]).astype(BF16)
    z = _dot(hb, win_ref[...])
    cq = _rms(z[:, 0:Q_RANK], qag_ref[...])
    ckv = _rms(z[:, Q_RANK:Q_RANK + KV_RANK], kvg_ref[...])
    zr = z[:, Q_RANK + KV_RANK:Q_RANK + KV_RANK + 128]
    kr = zr * lax.rsqrt(jnp.sum(zr * zr, axis=-1, keepdims=True) * (1.0 / ROPE_C) + EPS) * rkg_ref[...]
    lane = lax.broadcasted_iota(jnp.int32, (tm, 128), 1)
    half = ROPE_C // 2
    rot_k = jnp.where(lane < half, -pltpu.roll(kr, 128 - half, 1),
                      jnp.where(lane < ROPE_C, pltpu.roll(kr, half, 1), 0.0))
    krf = kr * ck_ref[...] + rot_k * sk_ref[...]
    rows_ref[:, 0:KV_RANK] = ckv
    rows_ref[:, KV_RANK:LAT_C] = krf[:, 0:ROPE_C]

    cqb = cq.astype(BF16)
    ckvb = ckv.astype(BF16)
    gm = gm_ref[...]
    kr_place = pltpu.roll(krf, NOPE_C, 1)
    kr2 = jnp.concatenate([kr_place, kr_place], axis=1)
    cos2 = jnp.concatenate([cq_ref[...], cq_ref[...]], axis=1)
    sin2 = jnp.concatenate([sq_ref[...], sq_ref[...]], axis=1)
    lane2 = lax.broadcasted_iota(jnp.int32, (tm, 256), 1) % 128
    first_half = jnp.logical_and(lane2 >= NOPE_C, lane2 < NOPE_C + half)
    second_half = jnp.logical_and(lane2 >= NOPE_C + half, lane2 < NOPE_C + ROPE_C)
    for c in range(H_C * HEAD_PAD // 256):
        cols = slice(c * 256, (c + 1) * 256)
        zq = _dot(cqb, wq_ref[:, cols])
        qn = zq * lax.rsqrt(_dot((zq * zq).astype(BF16), gm) + EPS) * gq_ref[:, cols]
        rot_q = jnp.where(first_half, -pltpu.roll(qn, 256 - half, 1),
                          jnp.where(second_half, pltpu.roll(qn, half, 1), 0.0))
        qh_ref[:, cols] = (qn * cos2 + rot_q * sin2).astype(BF16)
        zk = _dot(ckvb, wk_ref[:, cols])
        kn = zk * lax.rsqrt(_dot((zk * zk).astype(BF16), gm) + EPS) * gk_ref[:, cols] + kr2
        kh_ref[:, cols] = kn.astype(BF16)
    v_ref[...] = _dot(ckvb, wv_ref[...]).astype(BF16)


def _mla_proj(x, g, win, qag, kvg, rkg, wq, gq, gm, wk, gk, wv, tabs, *, tab_tiles, tm):
    n, d = x.shape
    tm = _row_tile(n, tm)
    hw = H_C * HEAD_PAD
    row = lambda c: pl.BlockSpec((tm, c), lambda i: (i, 0))
    tab = pl.BlockSpec((tm, 128), lambda i: (i % tab_tiles, 0))
    return pl.pallas_call(
        functools.partial(_mla_proj_kernel, tm=tm),
        grid=(n // tm,),
        in_specs=[row(d), _const((1, d)), _const((d, 512)), _const((1, Q_RANK)), _const((1, KV_RANK)),
                  _const((1, 128)), _const((Q_RANK, hw)), _const((1, hw)), _const((256, 256)),
                  _const((KV_RANK, hw)), _const((1, hw)), _const((KV_RANK, H_C * V_C)),
                  tab, tab, tab, tab],
        out_specs=[row(hw), row(hw), row(H_C * V_C), row(LAT_C)],
        out_shape=[jax.ShapeDtypeStruct((n, hw), BF16), jax.ShapeDtypeStruct((n, hw), BF16),
                   jax.ShapeDtypeStruct((n, H_C * V_C), BF16), jax.ShapeDtypeStruct((n, LAT_C), F32)],
        compiler_params=_params("parallel"),
        name="mla_proj",
    )(x, g, win, qag, kvg, rkg, wq, gq, gm, wk, gk, wv, *tabs)


N_PAGE_BUFS = 3


def _page_ring(pt_ref, cache_ref, buf, sem, *, rows_per_page, n_pages_step, n_steps, total):
    def copy(t, i, row0):
        return pltpu.make_async_copy(cache_ref.at[pl.ds(row0, rows_per_page), :],
                                     buf.at[t % N_PAGE_BUFS, pl.ds(i * rows_per_page, rows_per_page), :],
                                     sem.at[t % N_PAGE_BUFS])

    def start(t):
        tc = jnp.minimum(t, total - 1)
        seq, step = tc // n_steps, tc % n_steps
        for i in range(n_pages_step):
            page = pt_ref[seq, step * n_pages_step + i]
            copy(t, i, pl.multiple_of(page * rows_per_page, rows_per_page)).start()

    def wait(t):
        for i in range(n_pages_step):
            copy(t, i, 0).wait()

    return start, wait


def _diff_sample_kernel(pt_ref, qbd_ref, knew_ref, lam_ref, sg_ref, cache_ref, o_ref, m_s, l_s, acc_s, buf, sem,
                        *, n_pages_step, n_steps, n_seq, lam_init):
    c = pl.program_id(1)
    t = pl.program_id(0) * n_steps + c
    total = n_seq * n_steps
    start, wait = _page_ring(pt_ref, cache_ref, buf, sem, rows_per_page=PAGE * 8, n_pages_step=n_pages_step,
                             n_steps=n_steps, total=total)

    @pl.when(t == 0)
    def _():
        for k in range(N_PAGE_BUFS - 1):
            start(k)

    wait(t)
    start(t + N_PAGE_BUFS - 1)
    slot = t % N_PAGE_BUFS

    @pl.when(c == 0)
    def _():
        m_s[...] = jnp.full_like(m_s, NEG_BIG)
        l_s[...] = jnp.zeros_like(l_s)
        acc_s[...] = jnp.zeros_like(acc_s)

    def gather(i, first_row):
        return jnp.concatenate([buf[slot, pl.ds(i * PAGE * 8 + first_row + h, PAGE, stride=8), :]
                                for h in range(H_A)], axis=1).astype(BF16)

    qf = qbd_ref[0]
    qb = qf.astype(BF16)
    s = jnp.concatenate([_nt_dot(qb, gather(i, 0)) for i in range(n_pages_step)], axis=1)
    m_old = m_s[...]
    m_new = jnp.maximum(m_old, jnp.max(s, axis=-1, keepdims=True))
    alpha = jnp.exp(m_old - m_new)
    p = jnp.exp(s - m_new)
    l_s[...] = alpha * l_s[...] + jnp.sum(p, axis=-1, keepdims=True)
    pb = p.astype(BF16)
    pv = _dot(pb[:, 0:PAGE], gather(0, H_A))
    for i in range(1, n_pages_step):
        pv = pv + _dot(pb[:, i * PAGE:(i + 1) * PAGE], gather(i, H_A))
    acc_s[...] = alpha * acc_s[...] + pv
    m_s[...] = m_new

    @pl.when(c == n_steps - 1)
    def _():
        row = lax.broadcasted_iota(jnp.int32, (32, 1), 0)
        qidx = row % 4
        m0 = m_s[...]
        sj = []
        for j in range(4):
            kj = knew_ref[0, j:j + 1, 0:512]
            sj.append(jnp.where(j <= qidx, jnp.sum(qf * kj, axis=-1, keepdims=True), NEG_BIG))
        m1 = jnp.maximum(jnp.maximum(jnp.maximum(m0, sj[0]), jnp.maximum(sj[1], sj[2])), sj[3])
        al = jnp.exp(m0 - m1)
        l1 = al * l_s[...]
        acc = al * acc_s[...]
        for j in range(4):
            pj = jnp.exp(sj[j] - m1)
            l1 = l1 + pj
            acc = acc + pj * knew_ref[0, j:j + 1, 512:1024]
        accn = acc * (1.0 / l1)
        lv = lam_ref[...]
        lam = (jnp.exp(jnp.sum(lv[0:1] * lv[1:2], axis=-1, keepdims=True))
               - jnp.exp(jnp.sum(lv[2:3] * lv[3:4], axis=-1, keepdims=True)) + lam_init)
        o = accn[0:16] - lam * accn[16:32]
        lane_head = lax.broadcasted_iota(jnp.int32, (16, 512), 1) // 128
        row_head = lax.broadcasted_iota(jnp.int32, (16, 512), 0) // 4
        o = jnp.where(lane_head == row_head, o, 0.0)
        o = (o * lax.rsqrt(jnp.sum(o * o, axis=-1, keepdims=True) * (1.0 / 128.0) + EPS)
             * sg_ref[...] * (1.0 - lam_init))
        o_ref[0] = o[0:4] + o[4:8] + o[8:12] + o[12:16]

    @pl.when(t == total - 1)
    def _():
        for k in range(1, N_PAGE_BUFS):
            wait(t + k)


def _diff_sample(page_table, qbd, knew, lam, sg, cache, *, n_pages_step, lam_init):
    db, n_pages = page_table.shape
    n_steps = n_pages // n_pages_step
    grid_spec = pltpu.PrefetchScalarGridSpec(
        num_scalar_prefetch=1,
        grid=(db, n_steps),
        in_specs=[pl.BlockSpec((1, 32, 512), lambda s, c, pt: (s, 0, 0)),
                  pl.BlockSpec((1, 8, 1024), lambda s, c, pt: (s, 0, 0)),
                  pl.BlockSpec((4, HD_A), lambda s, c, pt: (0, 0)),
                  pl.BlockSpec((1, 512), lambda s, c, pt: (0, 0)),
                  pl.BlockSpec(memory_space=pl.ANY)],
        out_specs=pl.BlockSpec((1, 4, 512), lambda s, c, pt: (s, 0, 0)),
        scratch_shapes=[pltpu.VMEM((32, 1), F32), pltpu.VMEM((32, 1), F32), pltpu.VMEM((32, 512), F32),
                        pltpu.VMEM((N_PAGE_BUFS, n_pages_step * PAGE * 8, 128), F32),
                        pltpu.SemaphoreType.DMA((N_PAGE_BUFS,))],
    )
    return pl.pallas_call(
        functools.partial(_diff_sample_kernel, n_pages_step=n_pages_step, n_steps=n_steps, n_seq=db,
                          lam_init=lam_init),
        grid_spec=grid_spec,
        out_shape=jax.ShapeDtypeStruct((db, 4, 512), F32),
        compiler_params=_params("arbitrary", "arbitrary"),
        name="diff_sample",
    )(page_table, qbd, knew, lam, sg, cache)


def _mla_sample_kernel(pt_ref, qbd_ref, qrope_ref, qrows_ref, knew_ref, cnew_ref, wkt_ref, wkg_ref, wv_ref,
                       cache_ref, o_ref, m_s, l_s, lat_s, qabs_s, buf, sem, *, n_pages_step, n_steps, n_seq):
    c = pl.program_id(1)
    t = pl.program_id(0) * n_steps + c
    total = n_seq * n_steps
    nk = n_pages_step * PAGE
    start, wait = _page_ring(pt_ref, cache_ref, buf, sem, rows_per_page=LAT_C, n_pages_step=n_pages_step,
                             n_steps=n_steps, total=total)

    @pl.when(t == 0)
    def _():
        for k in range(N_PAGE_BUFS - 1):
            start(k)

    wait(t)
    start(t + N_PAGE_BUFS - 1)
    slot = t % N_PAGE_BUFS

    @pl.when(c == 0)
    def _():
        m_s[...] = jnp.full_like(m_s, NEG_BIG)
        l_s[...] = jnp.zeros_like(l_s)
        lat_s[...] = jnp.zeros_like(lat_s)
        qabs_s[...] = _dot(qbd_ref[0], wkg_ref[...]).astype(BF16)

    ct = jnp.concatenate([buf[slot, i * LAT_C:i * LAT_C + KV_RANK, :] for i in range(n_pages_step)],
                         axis=1).astype(BF16)
    krt = jnp.concatenate([buf[slot, i * LAT_C + KV_RANK:(i + 1) * LAT_C, :] for i in range(n_pages_step)],
                          axis=1)
    krt = jnp.concatenate([krt, jnp.zeros((128 - ROPE_C, nk), F32)], axis=0).astype(BF16)
    kpt = _dot(wkt_ref[...], ct)
    msq = jnp.sum((kpt * kpt).reshape(H_C, NOPE_C, nk), axis=1) * (1.0 / NOPE_C)
    rinv = lax.rsqrt(msq + EPS)
    s = (_dot(qabs_s[...], ct) * jnp.concatenate([rinv] * 4, axis=0)
         + _dot(qrope_ref[0], krt))
    m_old = m_s[...]
    m_new = jnp.maximum(m_old, jnp.max(s, axis=-1, keepdims=True))
    alpha = jnp.exp(m_old - m_new)
    p = jnp.exp(s - m_new)
    l_s[...] = alpha * l_s[...] + jnp.sum(p, axis=-1, keepdims=True)
    lat_s[...] = alpha * lat_s[...] + _nt_dot(p.astype(BF16), ct)
    m_s[...] = m_new

    @pl.when(c == n_steps - 1)
    def _():
        qrows = qrows_ref[0]
        qidx = lax.broadcasted_iota(jnp.int32, (64, 1), 0) // H_C
        m0 = m_s[...]
        sj = []
        for j in range(4):
            kt = jnp.concatenate([knew_ref[0, j]] * 4, axis=0)
            sj.append(jnp.where(j <= qidx, jnp.sum(qrows * kt, axis=-1, keepdims=True), NEG_BIG))
        m1 = jnp.maximum(jnp.maximum(jnp.maximum(m0, sj[0]), jnp.maximum(sj[1], sj[2])), sj[3])
        al = jnp.exp(m0 - m1)
        l1 = al * l_s[...]
        lat = al * lat_s[...]
        for j in range(4):
            pj = jnp.exp(sj[j] - m1)
            l1 = l1 + pj
            lat = lat + pj * cnew_ref[0, j:j + 1, :]
        latn = (lat * (1.0 / l1)).astype(BF16)
        full = _dot(latn, wv_ref[...])
        lane_head = lax.broadcasted_iota(jnp.int32, (64, H_C * V_C), 1) // V_C
        row_head = lax.broadcasted_iota(jnp.int32, (64, H_C * V_C), 0) % H_C
        full = jnp.where(lane_head == row_head, full, 0.0)
        for q in range(4):
            o_ref[0, q:q + 1, :] = jnp.sum(full[q * H_C:(q + 1) * H_C], axis=0, keepdims=True)
        o_ref[0, 4:8, :] = jnp.zeros((4, H_C * V_C), F32)

    @pl.when(t == total - 1)
    def _():
        for k in range(1, N_PAGE_BUFS):
            wait(t + k)


def _mla_sample(page_table, qbd, qrope, qrows, knew, cnew, wkt, wkg, wv, cache, *, n_pages_step):
    db, n_pages = page_table.shape
    n_steps = n_pages // n_pages_step
    seq3 = lambda a, b: pl.BlockSpec((1, a, b), lambda s, c, pt: (s, 0, 0))
    cst = lambda a, b: pl.BlockSpec((a, b), lambda s, c, pt: (0, 0))
    grid_spec = pltpu.PrefetchScalarGridSpec(
        num_scalar_prefetch=1,
        grid=(db, n_steps),
        in_specs=[seq3(64, H_C * NOPE_C), seq3(64, 128), seq3(64, 128),
                  pl.BlockSpec((1, 4, H_C, 128), lambda s, c, pt: (s, 0, 0, 0)),
                  seq3(8, KV_RANK), cst(H_C * NOPE_C, KV_RANK), cst(H_C * NOPE_C, KV_RANK),
                  cst(KV_RANK, H_C * V_C), pl.BlockSpec(memory_space=pl.ANY)],
        out_specs=seq3(8, H_C * V_C),
        scratch_shapes=[pltpu.VMEM((64, 1), F32), pltpu.VMEM((64, 1), F32), pltpu.VMEM((64, KV_RANK), F32),
                        pltpu.VMEM((64, KV_RANK), BF16),
                        pltpu.VMEM((N_PAGE_BUFS, n_pages_step * LAT_C, PAGE), F32),
                        pltpu.SemaphoreType.DMA((N_PAGE_BUFS,))],
    )
    return pl.pallas_call(
        functools.partial(_mla_sample_kernel, n_pages_step=n_pages_step, n_steps=n_steps, n_seq=db),
        grid_spec=grid_spec,
        out_shape=jax.ShapeDtypeStruct((db, 8, H_C * V_C), F32),
        compiler_params=_params("arbitrary", "arbitrary"),
        name="mla_sample",
    )(page_table, qbd, qrope, qrows, knew, cnew, wkt, wkg, wv, cache)


def _group_mean_matrix(sizes, total):
    m = jnp.zeros((total, total), F32)
    off = 0
    for sz, live in sizes:
        if live:
            m = m.at[off:off + sz, off:off + sz].set(1.0 / sz)
        off += sz
    return m.astype(BF16)


def _rope_tables(pos, lane_offset):
    half = ROPE_C // 2
    inv = ROPE_THETA ** (-jnp.arange(half, dtype=F32) / half)
    ang = pos.astype(F32)[:, None] * inv[None, :]
    c2 = jnp.concatenate([jnp.cos(ang), jnp.cos(ang)], axis=1)
    s2 = jnp.concatenate([jnp.sin(ang), jnp.sin(ang)], axis=1)
    cos = jnp.ones((pos.shape[0], 128), F32).at[:, lane_offset:lane_offset + ROPE_C].set(c2)
    sin = jnp.zeros((pos.shape[0], 128), F32).at[:, lane_offset:lane_offset + ROPE_C].set(s2)
    return cos, sin


def kernel(x_prompt, x_sample, cache_kv_diff, cache_mla, state_lru_h, state_lru_conv, state_ffn_conv, page_table,
           norm_mix, norm_ffn, w_in_e, qk_norm_a, lambda_a, subln_a, conv_w_lru, conv_b_lru, gate_w_lru,
           gate_b_lru, lambda_lru, w_out_e, w_in_c, q_a_norm, kv_a_norm, w_uq, w_ukv, qn_c, rn_c, w_out_c,
           w_up, conv_w_ffn, conv_b_ffn, w_down):
    B, T, D = x_prompt.shape
    DB, S, _ = x_sample.shape
    n_pages = page_table.shape[1]
    n_past = n_pages * PAGE
    n_pool = cache_kv_diff.shape[1]
    depth = norm_mix.shape[0]
    NP, NS = B * T, S * DB
    pt = page_table.astype(jnp.int32)

    xp = x_prompt.reshape(NP, D)
    xs = jnp.swapaxes(x_sample, 0, 1).reshape(NS, D)

    def to_seq_major(a):
        return jnp.swapaxes(a.reshape((S, DB) + a.shape[1:]), 0, 1)

    pos_p = jnp.arange(T, dtype=jnp.int32)
    pos_s = jnp.repeat(n_past + jnp.arange(S, dtype=jnp.int32), DB)
    tabs_p = _rope_tables(pos_p, NOPE_C) + _rope_tables(pos_p, 0)
    tabs_s = _rope_tables(pos_s, NOPE_C) + _rope_tables(pos_s, 0)

    kv_p, kv_s, mla_p, mla_s = [], [], [], []
    lh_p, lh_s, lc_p, lc_s, fc_p, fc_s = [], [], [], [], [], []

    for l in range(depth):
        g_mix = norm_mix[l].reshape(1, D)
        if l % 2 == 0:
            e = l // 2
            lam_init = 0.8 - 0.6 * math.exp(-0.3 * l)
            w_in = w_in_e[e].astype(BF16)
            qs = jnp.tile(qk_norm_a[e, 0].reshape(1, 2 * HD_A), (1, H_A)) * (1.0 / math.sqrt(HD_A))
            ks = jnp.tile(qk_norm_a[e, 1].reshape(1, 2 * HD_A), (1, H_A))
            gm64 = _group_mean_matrix([(HD_A, True)] * (2 * H_A), 2 * H_A * HD_A)
            lam_v = lambda_a[e]
            sg = subln_a[e].reshape(1, 2 * HD_A)
            cw, cb = conv_w_lru[e], conv_b_lru[e].reshape(1, D_RNN)
            gw = jnp.zeros((2, 8, 64, 8, 64), F32)
            gw = gw.at[:, jnp.arange(8), :, jnp.arange(8), :].set(jnp.swapaxes(gate_w_lru[e], 0, 1))
            gw = jnp.concatenate([gw[0].reshape(D_RNN, D_RNN), gw[1].reshape(D_RNN, D_RNN)], axis=1).astype(BF16)
            gb = gate_b_lru[e].reshape(1, 2 * D_RNN)
            lam_lru = lambda_lru[e].reshape(1, D_RNN)
            w_out = w_out_e[e].astype(BF16)

            qb_p, kb_p, vb_p, kvf_p, xg_p = _even_proj(xp, g_mix, w_in, qs, ks, gm64, 512)
            qb_s, kb_s, vb_s, kvf_s, xg_s = _even_proj(xs, g_mix, w_in, qs, ks, gm64, 512)

            a_p = _flash(qb_p, kb_p, vb_p, (lam_v, sg), diff=True, batch=B, seq=T, tb=512, npair=2, lam_init=lam_init)

            q_sm = to_seq_major(qb_s.astype(F32)).reshape(DB, S, H_A, 2, HD_A)
            eye_h = jnp.eye(H_A, dtype=F32)
            eye_m = jnp.eye(2, dtype=F32)
            qbd = jnp.einsum('bqhmd,hg,mn->bmhqgnd', q_sm, eye_h, eye_m)
            qbd = qbd.reshape(DB, 2 * H_A * S, H_A * 2 * HD_A)
            knew = jnp.pad(to_seq_major(kvf_s), ((0, 0), (0, 8 - S), (0, 0)))
            cache_d = cache_kv_diff.reshape(cache_kv_diff.shape[0] * n_pool * PAGE * 2 * H_A, 2 * HD_A)
            sg4 = jnp.tile(sg, (1, H_A))
            a_s = _diff_sample(pt + e * n_pool, qbd, knew, lam_v, sg4, cache_d,
                               n_pages_step=min(16, n_pages), lam_init=lam_init)
            a_s = jnp.swapaxes(a_s, 0, 1).reshape(NS, H_A * 2 * HD_A).astype(BF16)

            y_p, hl_pp, ct_pp = _lru_prompt(xg_p, cw, cb, gw, gb, lam_lru, batch=B, seq=T, tm=256)
            hist = jnp.swapaxes(state_lru_conv[e], 0, 1).reshape(3 * DB, D_RNN)
            y_s, hl_ss, ct_ss = _lru_sample(xg_s, hist, state_lru_h[e], cw, cb, gw, gb, lam_lru, db=DB, steps=S)

            xp = _mm_res(xp, [a_p, y_p], w_out, 512)
            xs = _mm_res(xs, [a_s, y_s], w_out, 512)

            kv_p.append(kvf_p.reshape(B, T, 2, H_A, 2 * HD_A))
            kv_s.append(to_seq_major(kvf_s).reshape(DB, S, 2, H_A, 2 * HD_A))
            nt = hl_pp.shape[0] // B
            lh_p.append(hl_pp.reshape(B, nt, 8, D_RNN)[:, -1, -1])
            lc_p.append(ct_pp.reshape(B, nt, 8, D_RNN)[:, -1, 5:8])
            lh_s.append(hl_ss)
            lc_s.append(jnp.swapaxes(ct_ss.reshape(3, DB, D_RNN), 0, 1))
        else:
            o = l // 2
            scale = 1.0 / math.sqrt(NOPE_C + ROPE_C)
            pad_h = HEAD_PAD - NOPE_C - ROPE_C
            win = jnp.pad(w_in_c[o], ((0, 0), (0, 512 - w_in_c.shape[2]))).astype(BF16)
            qag = q_a_norm[o].reshape(1, Q_RANK)
            kvg = kv_a_norm[o].reshape(1, KV_RANK)
            rkg = jnp.pad(rn_c[o, 1], (0, 128 - ROPE_C)).reshape(1, 128)
            wq = jnp.pad(w_uq[o], ((0, 0), (0, 0), (0, pad_h))).reshape(Q_RANK, H_C * HEAD_PAD).astype(BF16)
            gq = jnp.tile(jnp.concatenate([qn_c[o, 0], rn_c[o, 0], jnp.zeros((pad_h,), F32)]) * scale,
                          H_C).reshape(1, H_C * HEAD_PAD)
            gm256 = _group_mean_matrix([(NOPE_C, True), (ROPE_C, True), (pad_h, False)] * 2, 256)
            wk_nope = w_ukv[o][:, :, :NOPE_C]
            wk = jnp.pad(wk_nope, ((0, 0), (0, 0), (0, HEAD_PAD - NOPE_C))).reshape(KV_RANK, H_C * HEAD_PAD)
            wk = wk.astype(BF16)
            gk = jnp.tile(jnp.concatenate([qn_c[o, 1], jnp.zeros((HEAD_PAD - NOPE_C,), F32)]),
                          H_C).reshape(1, H_C * HEAD_PAD)
            wv = w_ukv[o][:, :, NOPE_C:].reshape(KV_RANK, H_C * V_C).astype(BF16)
            wkt = jnp.transpose(wk_nope.reshape(KV_RANK, H_C * NOPE_C))
            wkg = (wkt * jnp.tile(qn_c[o, 1], H_C)[:, None]).astype(BF16)
            wkt = wkt.astype(BF16)
            w_out = w_out_c[o].astype(BF16)

            tm_p = _row_tile(T, 512)
            qh_p, kh_p, v_p, rows_p = _mla_proj(xp, g_mix, win, qag, kvg, rkg, wq, gq, gm256, wk, gk, wv, tabs_p,
                                                tab_tiles=T // tm_p, tm=tm_p)
            qh_s, kh_s, v_s, rows_s = _mla_proj(xs, g_mix, win, qag, kvg, rkg, wq, gq, gm256, wk, gk, wv, tabs_s,
                                                tab_tiles=1, tm=NS)

            o_p = _flash(qh_p, kh_p, v_p, (), diff=False, batch=B, seq=T, tb=512, npair=2)

            qh_sm = to_seq_major(qh_s.astype(F32)).reshape(DB, S, H_C, HEAD_PAD)
            qrows = qh_sm.reshape(DB, S * H_C, HEAD_PAD)
            qn = qh_sm[..., :NOPE_C]
            qbd = jnp.einsum('bqhd,hg->bqhgd', qn, jnp.eye(H_C, dtype=F32)).reshape(DB, S * H_C, H_C * NOPE_C)
            qrope = jnp.pad(qh_sm[..., NOPE_C:NOPE_C + ROPE_C],
                            ((0, 0), (0, 0), (0, 0), (0, 128 - ROPE_C))).reshape(DB, S * H_C, 128)
            knew = to_seq_major(kh_s.astype(F32)).reshape(DB, S, H_C, HEAD_PAD)
            rows_sm = to_seq_major(rows_s)
            cnew = jnp.pad(rows_sm[..., :KV_RANK], ((0, 0), (0, 8 - S), (0, 0)))
            cache_m = jnp.swapaxes(cache_mla, 2, 3).reshape(cache_mla.shape[0] * n_pool * LAT_C, PAGE)
            o_s = _mla_sample(pt + o * n_pool, qbd.astype(BF16), qrope.astype(BF16), qrows, knew, cnew,
                              wkt, wkg, wv, cache_m, n_pages_step=min(16, n_pages))
            o_s = jnp.swapaxes(o_s[:, :S], 0, 1).reshape(NS, H_C * V_C).astype(BF16)

            xp = _mm_res(xp, [o_p], w_out, 512)
            xs = _mm_res(xs, [o_s], w_out, 512)
            mla_p.append(rows_p.reshape(B, T, LAT_C))
            mla_s.append(rows_sm)

        g_ffn = norm_ffn[l].reshape(1, D)
        wu = w_up[l].astype(BF16)
        wd = w_down[l].astype(BF16)
        cwf, cbf = conv_w_ffn[l], conv_b_ffn[l].reshape(1, 2 * D_FF)
        xp, tg, tl = _ffn_prompt(xp, g_ffn, wu, cwf, cbf, wd, seq=T, tm=1024, tf=1408)
        ntf = tg.shape[0] // B
        tail = jnp.concatenate([tg, tl], axis=-1).reshape(B, ntf, 8, 2 * D_FF)
        fc_p.append(tail[:, -1, 6:8])
        hist_f = jnp.swapaxes(state_ffn_conv[l], 0, 1).reshape(2 * DB, 2 * D_FF)
        xs, tgs, tls = _ffn_sample(xs, g_ffn, wu, cwf, cbf, wd, hist_f, db=DB, steps=S, tf=1408)
        fc_s.append(jnp.swapaxes(jnp.concatenate([tgs, tls], axis=-1).reshape(2, DB, 2 * D_FF), 0, 1))

    y_p = xp.reshape(B, T, D)
    y_s = to_seq_major(xs)
    return (y_p, y_s, jnp.stack(kv_p), jnp.stack(kv_s), jnp.stack(mla_p), jnp.stack(mla_s),
            jnp.stack(lh_p), jnp.stack(lh_s), jnp.stack(lc_p), jnp.stack(lc_s), jnp.stack(fc_p), jnp.stack(fc_s))
```

```python
import functools
import math

import jax
import jax.numpy as jnp
from jax import lax
from jax.experimental import pallas as pl
from jax.experimental.pallas import tpu as pltpu

F32 = jnp.float32
BF16 = jnp.bfloat16

EPS = 1e-6
NEG_BIG = -1e30
ROPE_THETA = 10000.0
LRU_C = 8.0
PAGE = 128

H_A = 4
HD_A = 64
D_RNN = 512
H_C = 16
Q_RANK = 256
KV_RANK = 128
NOPE_C = 64
ROPE_C = 32
V_C = 64
LAT_C = KV_RANK + ROPE_C
HEAD_PAD = 128
D_FF = 2816

VMEM_LIMIT_BYTES = 56 * 1024 * 1024


def _params(*semantics):
    return pltpu.CompilerParams(dimension_semantics=semantics,
                                vmem_limit_bytes=VMEM_LIMIT_BYTES)


def _row_tile(n, want):
    t = min(n, want)
    assert n % t == 0
    return t


def _const(shape):
    return pl.BlockSpec(shape, lambda *_: (0,) * len(shape))


def _rms(x, g):
    return x * lax.rsqrt(jnp.mean(x * x, axis=-1, keepdims=True) + EPS) * g


def _nt_dot(a, b):
    return lax.dot_general(a, b, (((1,), (1,)), ((), ())), preferred_element_type=F32)


def _dot(a, b):
    return jnp.dot(a, b, preferred_element_type=F32)


def _gelu_tanh(x):
    return 0.5 * x * (1.0 + jnp.tanh(math.sqrt(2.0 / math.pi) * (x + 0.044715 * x * x * x)))


def _softplus(y):
    return jnp.maximum(y, 0.0) + jnp.log1p(jnp.exp(-jnp.abs(y)))


def _shift_rows(x, prev8, s):
    xr = pltpu.roll(x, s, 0)
    row8 = lax.broadcasted_iota(jnp.int32, prev8.shape, 0)
    top = jnp.where(row8 < s, pltpu.roll(prev8, s, 0), xr[0:8])
    return jnp.concatenate([top, xr[8:]], axis=0)


def _even_proj_kernel(x_ref, g_ref, w_ref, qs_ref, ks_ref, gm_ref,
                      qb_ref, kb_ref, vb_ref, kv_ref, xg_ref):
    hb = _rms(x_ref[...], g_ref[...]).astype(BF16)
    z = _dot(hb, w_ref[...])
    zq, zk, zv = z[:, 0:512], z[:, 512:1024], z[:, 1024:1536]
    gm = gm_ref[...]
    mq = _dot((zq * zq).astype(BF16), gm)
    mk = _dot((zk * zk).astype(BF16), gm)
    q = zq * lax.rsqrt(mq + EPS) * qs_ref[...]
    k = zk * lax.rsqrt(mk + EPS) * ks_ref[...]
    qb_ref[...] = q.astype(BF16)
    kb_ref[...] = k.astype(BF16)
    vb_ref[...] = zv.astype(BF16)
    tm = k.shape[0]
    for h in range(H_A):
        kv_ref[pl.ds(h, tm, stride=2 * H_A), :] = k[:, h * 128:(h + 1) * 128]
        kv_ref[pl.ds(H_A + h, tm, stride=2 * H_A), :] = zv[:, h * 128:(h + 1) * 128]
    xg_ref[...] = z[:, 1536:2560]


def _even_proj(x, g, w, qs, ks, gm, tm):
    n = x.shape[0]
    tm = _row_tile(n, tm)
    row = lambda c: pl.BlockSpec((tm, c), lambda i: (i, 0))
    return pl.pallas_call(
        _even_proj_kernel,
        grid=(n // tm,),
        in_specs=[row(1024), _const((1, 1024)), _const((1024, 2560)), _const((1, 512)),
                  _const((1, 512)), _const((512, 512))],
        out_specs=[row(512), row(512), row(512), pl.BlockSpec((tm * 2 * H_A, 128), lambda i: (i, 0)), row(1024)],
        out_shape=[jax.ShapeDtypeStruct((n, 512), BF16), jax.ShapeDtypeStruct((n, 512), BF16),
                   jax.ShapeDtypeStruct((n, 512), BF16), jax.ShapeDtypeStruct((n * 2 * H_A, 128), F32),
                   jax.ShapeDtypeStruct((n, 1024), F32)],
        compiler_params=_params("parallel"),
        name="even_proj",
    )(x, g, w, qs, ks, gm)


def _online_update(s, m, acc, vext):
    m_new = jnp.maximum(m, jnp.max(s, axis=-1, keepdims=True))
    alpha = jnp.exp(m - m_new)
    p = jnp.exp(s - m_new).astype(BF16)
    return m_new, alpha * acc + _dot(p, vext)


def _flash_kernel(*refs, diff, tb, npair, lam_init):
    if diff:
        q_ref, k_ref, v_ref, lam_ref, sg_ref, o_ref = refs
    else:
        q_ref, k_ref, v_ref, o_ref = refs
    qi = pl.program_id(2)
    qw = 128 if diff else 256
    nchain = 2 * npair
    lane = lax.broadcasted_iota(jnp.int32, (tb, 128), 1)
    qs = []
    for g in range(npair):
        q = q_ref[:, g * qw:(g + 1) * qw]
        if diff:
            zero = jnp.zeros_like(q)
            qs.append(jnp.where(lane < HD_A, q, zero))
            qs.append(jnp.where(lane >= HD_A, q, zero))
        else:
            qs += [q[:, 0:128], q[:, 128:256]]

    ones = jnp.ones((tb, 128), BF16)

    def scores(j):
        start = pl.multiple_of(j * tb, tb)
        out = []
        for g in range(npair):
            kblk = k_ref[pl.ds(start, tb), g * qw:(g + 1) * qw]
            ks = [kblk, kblk] if diff else [kblk[:, 0:128], kblk[:, 128:256]]
            out += [_nt_dot(qs[2 * g + c], ks[c]) for c in range(2)]
        return out

    def values(j):
        start = pl.multiple_of(j * tb, tb)
        out = []
        for g in range(npair):
            v = v_ref[pl.ds(start, tb), g * 128:(g + 1) * 128]
            if diff:
                vext = jnp.concatenate([v, ones], axis=1)
                out += [vext, vext]
            else:
                out += [jnp.where(lane < V_C, v, ones), jnp.where(lane < V_C, ones, v)]
        return out

    def body(j, carry):
        s, vs = scores(j), values(j)
        out = []
        for c in range(nchain):
            out += list(_online_update(s[c], carry[2 * c], carry[2 * c + 1], vs[c]))
        return tuple(out)

    col1 = jnp.full((tb, 1), NEG_BIG, F32)
    zacc = jnp.zeros((tb, 256 if diff else 128), F32)
    carry = lax.fori_loop(0, qi, body, (col1, zacc) * nchain)

    s, vs = scores(qi), values(qi)
    r = lax.broadcasted_iota(jnp.int32, (tb, tb), 0)
    c = lax.broadcasted_iota(jnp.int32, (tb, tb), 1)
    keep = c <= r
    for g in range(npair):
        os_ = []
        for c2 in (2 * g, 2 * g + 1):
            _, a = _online_update(jnp.where(keep, s[c2], NEG_BIG), carry[2 * c2], carry[2 * c2 + 1], vs[c2])
            if diff:
                os_.append(a[:, 0:128] * (1.0 / a[:, 128:129]))
            else:
                den = a[:, 64:65] if c2 % 2 == 0 else a[:, 0:1]
                os_.append(a * (1.0 / den))
        if diff:
            lv = lam_ref[...]
            lam = (jnp.exp(jnp.sum(lv[0:1] * lv[1:2], axis=-1, keepdims=True))
                   - jnp.exp(jnp.sum(lv[2:3] * lv[3:4], axis=-1, keepdims=True)) + lam_init)
            o = os_[0] - lam * os_[1]
            o = _rms(o, sg_ref[...]) * (1.0 - lam_init)
        else:
            o = jnp.where(lane < V_C, os_[0], os_[1])
        o_ref[:, g * 128:(g + 1) * 128] = o.astype(o_ref.dtype)


def _flash(q, k, v, extra, *, diff, batch, seq, tb, npair, lam_init=0.0):
    n = q.shape[0]
    tb = _row_tile(seq, tb)
    nq = seq // tb
    qw = (128 if diff else 256) * npair
    groups = q.shape[1] // qw
    in_specs = [pl.BlockSpec((tb, qw), lambda b, g, i: (b * nq + i, g)),
                pl.BlockSpec((seq, qw), lambda b, g, i: (b, g)),
                pl.BlockSpec((seq, 128 * npair), lambda b, g, i: (b, g))]
    in_specs += [_const(e.shape) for e in extra]
    return pl.pallas_call(
        functools.partial(_flash_kernel, diff=diff, tb=tb, npair=npair, lam_init=lam_init),
        grid=(batch, groups, nq),
        in_specs=in_specs,
        out_specs=pl.BlockSpec((tb, 128 * npair), lambda b, g, i: (b * nq + i, g)),
        out_shape=jax.ShapeDtypeStruct((n, groups * 128 * npair), BF16),
        compiler_params=_params("parallel", "parallel", "arbitrary"),
        name="flash_diff" if diff else "flash_mla",
    )(q, k, v, *extra)


def _lru_gates(xc, gw_ref, gb_ref, lam_ref):
    g = _dot(xc.astype(BF16), gw_ref[...]) + gb_ref[...]
    r = jax.nn.sigmoid(g[:, 0:D_RNN])
    i = jax.nn.sigmoid(g[:, D_RNN:2 * D_RNN])
    a = jnp.exp(-LRU_C * r * _softplus(-lam_ref[...]))
    mult = jnp.sqrt(1.0 - a * a)
    return a, mult, i


def _lru_prompt_kernel(x_ref, gr_ref, cw_ref, cb_ref, gw_ref, gb_ref, lam_ref,
                       y_ref, hl_ref, ct_ref, xprev_s, h_s, *, tm):
    ti = pl.program_id(1)

    @pl.when(ti == 0)
    def _():
        xprev_s[...] = jnp.zeros_like(xprev_s)
        h_s[...] = jnp.zeros_like(h_s)

    x = x_ref[...]
    prev8 = xprev_s[...]
    cw = cw_ref[...]
    xc = (cb_ref[...] + cw[0:1] * _shift_rows(x, prev8, 3) + cw[1:2] * _shift_rows(x, prev8, 2)
          + cw[2:3] * _shift_rows(x, prev8, 1) + cw[3:4] * x)
    a, mult, i = _lru_gates(xc, gw_ref, gb_ref, lam_ref)
    row = lax.broadcasted_iota(jnp.int32, (tm, D_RNN), 0)
    mult = jnp.where(jnp.logical_and(row == 0, ti == 0), 1.0, mult)
    b = mult * i * xc
    d = 1
    while d < tm:
        ar = jnp.where(row >= d, pltpu.roll(a, d, 0), 1.0)
        br = jnp.where(row >= d, pltpu.roll(b, d, 0), 0.0)
        b = a * br + b
        a = a * ar
        d *= 2
    h = a * h_s[7:8] + b
    y_ref[...] = (h * _gelu_tanh(gr_ref[...])).astype(BF16)
    h_s[...] = h[tm - 8:tm]
    hl_ref[0] = h[tm - 8:tm]
    ct_ref[0] = x[tm - 8:tm]
    xprev_s[...] = x[tm - 8:tm]


def _lru_prompt(xg, cw, cb, gw, gb, lam, *, batch, seq, tm):
    n = xg.shape[0]
    tm = _row_tile(seq, tm)
    nt = seq // tm
    return pl.pallas_call(
        functools.partial(_lru_prompt_kernel, tm=tm),
        grid=(batch, nt),
        in_specs=[pl.BlockSpec((tm, D_RNN), lambda b, t: (b * nt + t, 0)),
                  pl.BlockSpec((tm, D_RNN), lambda b, t: (b * nt + t, 1)),
                  _const((4, D_RNN)), _const((1, D_RNN)), _const((D_RNN, 2 * D_RNN)),
                  _const((1, 2 * D_RNN)), _const((1, D_RNN))],
        out_specs=[pl.BlockSpec((tm, D_RNN), lambda b, t: (b * nt + t, 0)),
                   pl.BlockSpec((1, 8, D_RNN), lambda b, t: (b * nt + t, 0, 0)),
                   pl.BlockSpec((1, 8, D_RNN), lambda b, t: (b * nt + t, 0, 0))],
        out_shape=[jax.ShapeDtypeStruct((n, D_RNN), BF16),
                   jax.ShapeDtypeStruct((batch * nt, 8, D_RNN), F32),
                   jax.ShapeDtypeStruct((batch * nt, 8, D_RNN), F32)],
        scratch_shapes=[pltpu.VMEM((8, D_RNN), F32), pltpu.VMEM((8, D_RNN), F32)],
        compiler_params=_params("arbitrary", "arbitrary"),
        name="lru_prompt",
    )(xg, xg, cw, cb, gw, gb, lam)


def _lru_sample_kernel(x_ref, gr_ref, hist_ref, h0_ref, cw_ref, cb_ref, gw_ref, gb_ref, lam_ref,
                       y_ref, hl_ref, ct_ref, *, db, steps):
    n = steps * db
    ext = jnp.concatenate([hist_ref[...], x_ref[...]], axis=0)
    cw = cw_ref[...]
    xc = (cb_ref[...] + cw[0:1] * ext[0:n] + cw[1:2] * ext[db:db + n]
          + cw[2:3] * ext[2 * db:2 * db + n] + cw[3:4] * ext[3 * db:3 * db + n])
    a, mult, i = _lru_gates(xc, gw_ref, gb_ref, lam_ref)
    b = mult * i * xc
    h = h0_ref[...]
    hs = []
    for t in range(steps):
        h = a[t * db:(t + 1) * db] * h + b[t * db:(t + 1) * db]
        hs.append(h)
    hseq = jnp.concatenate(hs, axis=0)
    y_ref[...] = (hseq * _gelu_tanh(gr_ref[...])).astype(BF16)
    hl_ref[...] = h
    ct_ref[...] = ext[n:n + 3 * db]


def _lru_sample(xg, hist, h0, cw, cb, gw, gb, lam, *, db, steps):
    n = steps * db
    return pl.pallas_call(
        functools.partial(_lru_sample_kernel, db=db, steps=steps),
        grid=(1,),
        in_specs=[pl.BlockSpec((n, D_RNN), lambda i: (0, 0)),
                  pl.BlockSpec((n, D_RNN), lambda i: (0, 1)),
                  _const((3 * db, D_RNN)), _const((db, D_RNN)),
                  _const((4, D_RNN)), _const((1, D_RNN)), _const((D_RNN, 2 * D_RNN)),
                  _const((1, 2 * D_RNN)), _const((1, D_RNN))],
        out_specs=[_const((n, D_RNN)), _const((db, D_RNN)), _const((3 * db, D_RNN))],
        out_shape=[jax.ShapeDtypeStruct((n, D_RNN), BF16),
                   jax.ShapeDtypeStruct((db, D_RNN), F32),
                   jax.ShapeDtypeStruct((3 * db, D_RNN), F32)],
        compiler_params=_params("arbitrary"),
        name="lru_sample",
    )(xg, xg, hist, h0, cw, cb, gw, gb, lam)


def _mm_res_kernel(*refs, n_in):
    res_ref, o_ref = refs[0], refs[-1]
    acc = res_ref[...]
    for a_ref, w_ref in zip(refs[1:1 + n_in], refs[1 + n_in:1 + 2 * n_in]):
        acc = acc + _dot(a_ref[...], w_ref[...])
    o_ref[...] = acc


def _mm_res(res, acts, w, tm):
    n, d = res.shape
    tm = _row_tile(n, tm)
    n_in = len(acts)
    kw = w.shape[0] // n_in
    in_specs = [pl.BlockSpec((tm, d), lambda i: (i, 0))]
    in_specs += [pl.BlockSpec((tm, kw), lambda i: (i, 0)) for _ in acts]
    in_specs += [pl.BlockSpec((kw, d), lambda i, j=j: (j, 0)) for j in range(n_in)]
    return pl.pallas_call(
        functools.partial(_mm_res_kernel, n_in=n_in),
        grid=(n // tm,),
        in_specs=in_specs,
        out_specs=pl.BlockSpec((tm, d), lambda i: (i, 0)),
        out_shape=jax.ShapeDtypeStruct((n, d), F32),
        compiler_params=_params("parallel"),
        name="out_proj",
    )(res, *acts, *([w] * n_in))


def _ffn_prompt_kernel(x_ref, g_ref, wg_ref, wl_ref, cwg_ref, cwl_ref, cbg_ref, cbl_ref, wd_ref,
                       o_ref, tg_ref, tl_ref, hb_s, cg_s, cl_s, *, tm, tf, tiles_per_seq):
    i = pl.program_id(0)
    j = pl.program_id(1)

    @pl.when(j == 0)
    def _():
        hb_s[...] = _rms(x_ref[...], g_ref[...]).astype(BF16)
        o_ref[...] = x_ref[...]

    @pl.when((i % tiles_per_seq) == 0)
    def _():
        cg_s[j] = jnp.zeros((8, tf), F32)
        cl_s[j] = jnp.zeros((8, tf), F32)

    hb = hb_s[...]
    prev_g, prev_l = cg_s[j], cl_s[j]
    acts, tails_g, tails_l = [], [], []
    for c0 in range(0, tf, 256):
        cols = slice(c0, min(c0 + 256, tf))

        def conv(w_ref, cw_ref, cb_ref, prev):
            u = _dot(hb, w_ref[:, cols])
            cw = cw_ref[:, cols]
            p8 = prev[:, cols]
            c = cb_ref[:, cols] + cw[0:1] * _shift_rows(u, p8, 2) + cw[1:2] * _shift_rows(u, p8, 1) + cw[2:3] * u
            return c, u[tm - 8:tm]

        cg, tg = conv(wg_ref, cwg_ref, cbg_ref, prev_g)
        cl, tl = conv(wl_ref, cwl_ref, cbl_ref, prev_l)
        tails_g.append(tg)
        tails_l.append(tl)
        acts.append((cg * jax.nn.sigmoid(cg) * cl).astype(BF16))
    tail_g = jnp.concatenate(tails_g, axis=1)
    tail_l = jnp.concatenate(tails_l, axis=1)
    cg_s[j] = tail_g
    cl_s[j] = tail_l
    tg_ref[0] = tail_g
    tl_ref[0] = tail_l
    o_ref[...] += _dot(jnp.concatenate(acts, axis=1), wd_ref[...])


def _ffn_prompt(x, g, w_up, cw, cb, w_down, *, seq, tm, tf):
    n, d = x.shape
    tm = _row_tile(seq, tm)
    nf = D_FF // tf
    nt = n // tm
    return pl.pallas_call(
        functools.partial(_ffn_prompt_kernel, tm=tm, tf=tf, tiles_per_seq=seq // tm),
        grid=(nt, nf),
        in_specs=[pl.BlockSpec((tm, d), lambda i, j: (i, 0)), _const((1, d)),
                  pl.BlockSpec((d, tf), lambda i, j: (0, j)),
                  pl.BlockSpec((d, tf), lambda i, j: (0, j + nf)),
                  pl.BlockSpec((3, tf), lambda i, j: (0, j)),
                  pl.BlockSpec((3, tf), lambda i, j: (0, j + nf)),
                  pl.BlockSpec((1, tf), lambda i, j: (0, j)),
                  pl.BlockSpec((1, tf), lambda i, j: (0, j + nf)),
                  pl.BlockSpec((tf, d), lambda i, j: (j, 0))],
        out_specs=[pl.BlockSpec((tm, d), lambda i, j: (i, 0)),
                   pl.BlockSpec((1, 8, tf), lambda i, j: (i, 0, j)),
                   pl.BlockSpec((1, 8, tf), lambda i, j: (i, 0, j))],
        out_shape=[jax.ShapeDtypeStruct((n, d), F32),
                   jax.ShapeDtypeStruct((nt, 8, D_FF), F32),
                   jax.ShapeDtypeStruct((nt, 8, D_FF), F32)],
        scratch_shapes=[pltpu.VMEM((tm, d), BF16), pltpu.VMEM((nf, 8, tf), F32),
                        pltpu.VMEM((nf, 8, tf), F32)],
        compiler_params=_params("arbitrary", "arbitrary"),
        name="ffn_prompt",
    )(x, g, w_up, w_up, cw, cw, cb, cb, w_down)


def _ffn_sample_kernel(x_ref, g_ref, wg_ref, wl_ref, cwg_ref, cwl_ref, cbg_ref, cbl_ref, wd_ref,
                       hg_ref, hl_ref, o_ref, tg_ref, tl_ref, hb_s, *, db, steps):
    j = pl.program_id(0)
    n = steps * db

    @pl.when(j == 0)
    def _():
        hb_s[...] = _rms(x_ref[...], g_ref[...]).astype(BF16)

    hb = hb_s[...]

    def branch(w_ref, cw_ref, cb_ref, hist_ref, tail_ref):
        u = _dot(hb, w_ref[...])
        ext = jnp.concatenate([hist_ref[...], u], axis=0)
        cw = cw_ref[...]
        c = cb_ref[...] + cw[0:1] * ext[0:n] + cw[1:2] * ext[db:db + n] + cw[2:3] * ext[2 * db:2 * db + n]
        tail_ref[...] = ext[n:n + 2 * db]
        return c

    cg = branch(wg_ref, cwg_ref, cbg_ref, hg_ref, tg_ref)
    cl = branch(wl_ref, cwl_ref, cbl_ref, hl_ref, tl_ref)
    act = (cg * jax.nn.sigmoid(cg) * cl).astype(BF16)
    contrib = _dot(act, wd_ref[...])

    @pl.when(j == 0)
    def _():
        o_ref[...] = x_ref[...] + contrib

    @pl.when(j > 0)
    def _():
        o_ref[...] += contrib


def _ffn_sample(x, g, w_up, cw, cb, w_down, hist, *, db, steps, tf):
    n, d = x.shape
    nf = D_FF // tf
    return pl.pallas_call(
        functools.partial(_ffn_sample_kernel, db=db, steps=steps),
        grid=(nf,),
        in_specs=[_const((n, d)), _const((1, d)),
                  pl.BlockSpec((d, tf), lambda j: (0, j)),
                  pl.BlockSpec((d, tf), lambda j: (0, j + nf)),
                  pl.BlockSpec((3, tf), lambda j: (0, j)),
                  pl.BlockSpec((3, tf), lambda j: (0, j + nf)),
                  pl.BlockSpec((1, tf), lambda j: (0, j)),
                  pl.BlockSpec((1, tf), lambda j: (0, j + nf)),
                  pl.BlockSpec((tf, d), lambda j: (j, 0)),
                  pl.BlockSpec((2 * db, tf), lambda j: (0, j)),
                  pl.BlockSpec((2 * db, tf), lambda j: (0, j + nf))],
        out_specs=[_const((n, d)),
                   pl.BlockSpec((2 * db, tf), lambda j: (0, j)),
                   pl.BlockSpec((2 * db, tf), lambda j: (0, j))],
        out_shape=[jax.ShapeDtypeStruct((n, d), F32),
                   jax.ShapeDtypeStruct((2 * db, D_FF), F32),
                   jax.ShapeDtypeStruct((2 * db, D_FF), F32)],
        scratch_shapes=[pltpu.VMEM((n, d), BF16)],
        compiler_params=_params("arbitrary"),
        name="ffn_sample",
    )(x, g, w_up, w_up, cw, cw, cb, cb, w_down, hist, hist)


def _mla_proj_kernel(x_ref, g_ref, win_ref, qag_ref, kvg_ref, rkg_ref, wq_ref, gq_ref, gm_ref,
                     wk_ref, gk_ref, wv_ref, cq_ref, sq_ref, ck_ref, sk_ref,
                     qh_ref, kh_ref, v_ref, rows_ref, *, tm):
    hb = _rms(x_ref[...], g_ref[...]).astype(BF16)
    z = _dot(hb, win_ref[...])
    cq = _rms(z[:, 0:Q_RANK], qag_ref[...])
    ckv = _rms(z[:, Q_RANK:Q_RANK + KV_RANK], kvg_ref[...])
    zr = z[:, Q_RANK + KV_RANK:Q_RANK + KV_RANK + 128]
    kr = zr * lax.rsqrt(jnp.sum(zr * zr, axis=-1, keepdims=True) * (1.0 / ROPE_C) + EPS) * rkg_ref[...]
    lane = lax.broadcasted_iota(jnp.int32, (tm, 128), 1)
    half = ROPE_C // 2
    rot_k = jnp.where(lane < half, -pltpu.roll(kr, 128 - half, 1),
                      jnp.where(lane < ROPE_C, pltpu.roll(kr, half, 1), 0.0))
    krf = kr * ck_ref[...] + rot_k * sk_ref[...]
    rows_ref[:, 0:KV_RANK] = ckv
    rows_ref[:, KV_RANK:LAT_C] = krf[:, 0:ROPE_C]

    cqb = cq.astype(BF16)
    ckvb = ckv.astype(BF16)
    gm = gm_ref[...]
    kr_place = pltpu.roll(krf, NOPE_C, 1)
    kr2 = jnp.concatenate([kr_place, kr_place], axis=1)
    cos2 = jnp.concatenate([cq_ref[...], cq_ref[...]], axis=1)
    sin2 = jnp.concatenate([sq_ref[...], sq_ref[...]], axis=1)
    lane2 = lax.broadcasted_iota(jnp.int32, (tm, 256), 1) % 128
    first_half = jnp.logical_and(lane2 >= NOPE_C, lane2 < NOPE_C + half)
    second_half = jnp.logical_and(lane2 >= NOPE_C + half, lane2 < NOPE_C + ROPE_C)
    for c in range(H_C * HEAD_PAD // 256):
        cols = slice(c * 256, (c + 1) * 256)
        zq = _dot(cqb, wq_ref[:, cols])
        qn = zq * lax.rsqrt(_dot((zq * zq).astype(BF16), gm) + EPS) * gq_ref[:, cols]
        rot_q = jnp.where(first_half, -pltpu.roll(qn, 256 - half, 1),
                          jnp.where(second_half, pltpu.roll(qn, half, 1), 0.0))
        qh_ref[:, cols] = (qn * cos2 + rot_q * sin2).astype(BF16)
        zk = _dot(ckvb, wk_ref[:, cols])
        kn = zk * lax.rsqrt(_dot((zk * zk).astype(BF16), gm) + EPS) * gk_ref[:, cols] + kr2
        kh_ref[:, cols] = kn.astype(BF16)
    v_ref[...] = _dot(ckvb, wv_ref[...]).astype(BF16)


def _mla_proj(x, g, win, qag, kvg, rkg, wq, gq, gm, wk, gk, wv, tabs, *, tab_tiles, tm):
    n, d = x.shape
    tm = _row_tile(n, tm)
    hw = H_C * HEAD_PAD
    row = lambda c: pl.BlockSpec((tm, c), lambda i: (i, 0))
    tab = pl.BlockSpec((tm, 128), lambda i: (i % tab_tiles, 0))
    return pl.pallas_call(
        functools.partial(_mla_proj_kernel, tm=tm),
        grid=(n // tm,),
        in_specs=[row(d), _const((1, d)), _const((d, 512)), _const((1, Q_RANK)), _const((1, KV_RANK)),
                  _const((1, 128)), _const((Q_RANK, hw)), _const((1, hw)), _const((256, 256)),
                  _const((KV_RANK, hw)), _const((1, hw)), _const((KV_RANK, H_C * V_C)),
                  tab, tab, tab, tab],
        out_specs=[row(hw), row(hw), row(H_C * V_C), row(LAT_C)],
        out_shape=[jax.ShapeDtypeStruct((n, hw), BF16), jax.ShapeDtypeStruct((n, hw), BF16),
                   jax.ShapeDtypeStruct((n, H_C * V_C), BF16), jax.ShapeDtypeStruct((n, LAT_C), F32)],
        compiler_params=_params("parallel"),
        name="mla_proj",
    )(x, g, win, qag, kvg, rkg, wq, gq, gm, wk, gk, wv, *tabs)


N_PAGE_BUFS = 3


def _page_ring(pt_ref, cache_ref, buf, sem, *, rows_per_page, n_pages_step, n_steps, total):
    def copy(t, i, row0):
        return pltpu.make_async_copy(cache_ref.at[pl.ds(row0, rows_per_page), :],
                                     buf.at[t % N_PAGE_BUFS, pl.ds(i * rows_per_page, rows_per_page), :],
                                     sem.at[t % N_PAGE_BUFS])

    def start(t):
        tc = jnp.minimum(t, total - 1)
        seq, step = tc // n_steps, tc % n_steps
        for i in range(n_pages_step):
            page = pt_ref[seq, step * n_pages_step + i]
            copy(t, i, pl.multiple_of(page * rows_per_page, rows_per_page)).start()

    def wait(t):
        for i in range(n_pages_step):
            copy(t, i, 0).wait()

    return start, wait


def _diff_sample_kernel(pt_ref, qbd_ref, knew_ref, lam_ref, sg_ref, cache_ref, o_ref, m_s, l_s, acc_s, buf, sem,
                        *, n_pages_step, n_steps, n_seq, lam_init):
    c = pl.program_id(1)
    t = pl.program_id(0) * n_steps + c
    total = n_seq * n_steps
    start, wait = _page_ring(pt_ref, cache_ref, buf, sem, rows_per_page=PAGE * 8, n_pages_step=n_pages_step,
                             n_steps=n_steps, total=total)

    @pl.when(t == 0)
    def _():
        for k in range(N_PAGE_BUFS - 1):
            start(k)

    wait(t)
    start(t + N_PAGE_BUFS - 1)
    slot = t % N_PAGE_BUFS

    @pl.when(c == 0)
    def _():
        m_s[...] = jnp.full_like(m_s, NEG_BIG)
        l_s[...] = jnp.zeros_like(l_s)
        acc_s[...] = jnp.zeros_like(acc_s)

    def gather(i, first_row):
        return jnp.concatenate([buf[slot, pl.ds(i * PAGE * 8 + first_row + h, PAGE, stride=8), :]
                                for h in range(H_A)], axis=1).astype(BF16)

    qf = qbd_ref[0]
    qb = qf.astype(BF16)
    s = jnp.concatenate([_nt_dot(qb, gather(i, 0)) for i in range(n_pages_step)], axis=1)
    m_old = m_s[...]
    m_new = jnp.maximum(m_old, jnp.max(s, axis=-1, keepdims=True))
    alpha = jnp.exp(m_old - m_new)
    p = jnp.exp(s - m_new)
    l_s[...] = alpha * l_s[...] + jnp.sum(p, axis=-1, keepdims=True)
    pb = p.astype(BF16)
    pv = _dot(pb[:, 0:PAGE], gather(0, H_A))
    for i in range(1, n_pages_step):
        pv = pv + _dot(pb[:, i * PAGE:(i + 1) * PAGE], gather(i, H_A))
    acc_s[...] = alpha * acc_s[...] + pv
    m_s[...] = m_new

    @pl.when(c == n_steps - 1)
    def _():
        row = lax.broadcasted_iota(jnp.int32, (32, 1), 0)
        qidx = row % 4
        m0 = m_s[...]
        sj = []
        for j in range(4):
            kj = knew_ref[0, j:j + 1, 0:512]
            sj.append(jnp.where(j <= qidx, jnp.sum(qf * kj, axis=-1, keepdims=True), NEG_BIG))
        m1 = jnp.maximum(jnp.maximum(jnp.maximum(m0, sj[0]), jnp.maximum(sj[1], sj[2])), sj[3])
        al = jnp.exp(m0 - m1)
        l1 = al * l_s[...]
        acc = al * acc_s[...]
        for j in range(4):
            pj = jnp.exp(sj[j] - m1)
            l1 = l1 + pj
            acc = acc + pj * knew_ref[0, j:j + 1, 512:1024]
        accn = acc * (1.0 / l1)
        lv = lam_ref[...]
        lam = (jnp.exp(jnp.sum(lv[0:1] * lv[1:2], axis=-1, keepdims=True))
               - jnp.exp(jnp.sum(lv[2:3] * lv[3:4], axis=-1, keepdims=True)) + lam_init)
        o = accn[0:16] - lam * accn[16:32]
        lane_head = lax.broadcasted_iota(jnp.int32, (16, 512), 1) // 128
        row_head = lax.broadcasted_iota(jnp.int32, (16, 512), 0) // 4
        o = jnp.where(lane_head == row_head, o, 0.0)
        o = (o * lax.rsqrt(jnp.sum(o * o, axis=-1, keepdims=True) * (1.0 / 128.0) + EPS)
             * sg_ref[...] * (1.0 - lam_init))
        o_ref[0] = o[0:4] + o[4:8] + o[8:12] + o[12:16]

    @pl.when(t == total - 1)
    def _():
        for k in range(1, N_PAGE_BUFS):
            wait(t + k)


def _diff_sample(page_table, qbd, knew, lam, sg, cache, *, n_pages_step, lam_init):
    db, n_pages = page_table.shape
    n_steps = n_pages // n_pages_step
    grid_spec = pltpu.PrefetchScalarGridSpec(
        num_scalar_prefetch=1,
        grid=(db, n_steps),
        in_specs=[pl.BlockSpec((1, 32, 512), lambda s, c, pt: (s, 0, 0)),
                  pl.BlockSpec((1, 8, 1024), lambda s, c, pt: (s, 0, 0)),
                  pl.BlockSpec((4, HD_A), lambda s, c, pt: (0, 0)),
                  pl.BlockSpec((1, 512), lambda s, c, pt: (0, 0)),
                  pl.BlockSpec(memory_space=pl.ANY)],
        out_specs=pl.BlockSpec((1, 4, 512), lambda s, c, pt: (s, 0, 0)),
        scratch_shapes=[pltpu.VMEM((32, 1), F32), pltpu.VMEM((32, 1), F32), pltpu.VMEM((32, 512), F32),
                        pltpu.VMEM((N_PAGE_BUFS, n_pages_step * PAGE * 8, 128), F32),
                        pltpu.SemaphoreType.DMA((N_PAGE_BUFS,))],
    )
    return pl.pallas_call(
        functools.partial(_diff_sample_kernel, n_pages_step=n_pages_step, n_steps=n_steps, n_seq=db,
                          lam_init=lam_init),
        grid_spec=grid_spec,
        out_shape=jax.ShapeDtypeStruct((db, 4, 512), F32),
        compiler_params=_params("arbitrary", "arbitrary"),
        name="diff_sample",
    )(page_table, qbd, knew, lam, sg, cache)


def _mla_sample_kernel(pt_ref, qbd_ref, qrope_ref, qrows_ref, knew_ref, cnew_ref, wkt_ref, wkg_ref, wv_ref,
                       cache_ref, o_ref, m_s, l_s, lat_s, qabs_s, buf, sem, *, n_pages_step, n_steps, n_seq):
    c = pl.program_id(1)
    t = pl.program_id(0) * n_steps + c
    total = n_seq * n_steps
    nk = n_pages_step * PAGE
    start, wait = _page_ring(pt_ref, cache_ref, buf, sem, rows_per_page=LAT_C, n_pages_step=n_pages_step,
                             n_steps=n_steps, total=total)

    @pl.when(t == 0)
    def _():
        for k in range(N_PAGE_BUFS - 1):
            start(k)

    wait(t)
    start(t + N_PAGE_BUFS - 1)
    slot = t % N_PAGE_BUFS

    @pl.when(c == 0)
    def _():
        m_s[...] = jnp.full_like(m_s, NEG_BIG)
        l_s[...] = jnp.zeros_like(l_s)
        lat_s[...] = jnp.zeros_like(lat_s)
        qabs_s[...] = _dot(qbd_ref[0], wkg_ref[...]).astype(BF16)

    ct = jnp.concatenate([buf[slot, i * LAT_C:i * LAT_C + KV_RANK, :] for i in range(n_pages_step)],
                         axis=1).astype(BF16)
    krt = jnp.concatenate([buf[slot, i * LAT_C + KV_RANK:(i + 1) * LAT_C, :] for i in range(n_pages_step)],
                          axis=1)
    krt = jnp.concatenate([krt, jnp.zeros((128 - ROPE_C, nk), F32)], axis=0).astype(BF16)
    kpt = _dot(wkt_ref[...], ct)
    msq = jnp.sum((kpt * kpt).reshape(H_C, NOPE_C, nk), axis=1) * (1.0 / NOPE_C)
    rinv = lax.rsqrt(msq + EPS)
    s = (_dot(qabs_s[...], ct) * jnp.concatenate([rinv] * 4, axis=0)
         + _dot(qrope_ref[0], krt))
    m_old = m_s[...]
    m_new = jnp.maximum(m_old, jnp.max(s, axis=-1, keepdims=True))
    alpha = jnp.exp(m_old - m_new)
    p = jnp.exp(s - m_new)
    l_s[...] = alpha * l_s[...] + jnp.sum(p, axis=-1, keepdims=True)
    lat_s[...] = alpha * lat_s[...] + _nt_dot(p.astype(BF16), ct)
    m_s[...] = m_new

    @pl.when(c == n_steps - 1)
    def _():
        qrows = qrows_ref[0]
        qidx = lax.broadcasted_iota(jnp.int32, (64, 1), 0) // H_C
        m0 = m_s[...]
        sj = []
        for j in range(4):
            kt = jnp.concatenate([knew_ref[0, j]] * 4, axis=0)
            sj.append(jnp.where(j <= qidx, jnp.sum(qrows * kt, axis=-1, keepdims=True), NEG_BIG))
        m1 = jnp.maximum(jnp.maximum(jnp.maximum(m0, sj[0]), jnp.maximum(sj[1], sj[2])), sj[3])
        al = jnp.exp(m0 - m1)
        l1 = al * l_s[...]
        lat = al * lat_s[...]
        for j in range(4):
            pj = jnp.exp(sj[j] - m1)
            l1 = l1 + pj
            lat = lat + pj * cnew_ref[0, j:j + 1, :]
        latn = (lat * (1.0 / l1)).astype(BF16)
        full = _dot(latn, wv_ref[...])
        lane_head = lax.broadcasted_iota(jnp.int32, (64, H_C * V_C), 1) // V_C
        row_head = lax.broadcasted_iota(jnp.int32, (64, H_C * V_C), 0) % H_C
        full = jnp.where(lane_head == row_head, full, 0.0)
        for q in range(4):
            o_ref[0, q:q + 1, :] = jnp.sum(full[q * H_C:(q + 1) * H_C], axis=0, keepdims=True)
        o_ref[0, 4:8, :] = jnp.zeros((4, H_C * V_C), F32)

    @pl.when(t == total - 1)
    def _():
        for k in range(1, N_PAGE_BUFS):
            wait(t + k)


def _mla_sample(page_table, qbd, qrope, qrows, knew, cnew, wkt, wkg, wv, cache, *, n_pages_step):
    db, n_pages = page_table.shape
    n_steps = n_pages // n_pages_step
    seq3 = lambda a, b: pl.BlockSpec((1, a, b), lambda s, c, pt: (s, 0, 0))
    cst = lambda a, b: pl.BlockSpec((a, b), lambda s, c, pt: (0, 0))
    grid_spec = pltpu.PrefetchScalarGridSpec(
        num_scalar_prefetch=1,
        grid=(db, n_steps),
        in_specs=[seq3(64, H_C * NOPE_C), seq3(64, 128), seq3(64, 128),
                  pl.BlockSpec((1, 4, H_C, 128), lambda s, c, pt: (s, 0, 0, 0)),
                  seq3(8, KV_RANK), cst(H_C * NOPE_C, KV_RANK), cst(H_C * NOPE_C, KV_RANK),
                  cst(KV_RANK, H_C * V_C), pl.BlockSpec(memory_space=pl.ANY)],
        out_specs=seq3(8, H_C * V_C),
        scratch_shapes=[pltpu.VMEM((64, 1), F32), pltpu.VMEM((64, 1), F32), pltpu.VMEM((64, KV_RANK), F32),
                        pltpu.VMEM((64, KV_RANK), BF16),
                        pltpu.VMEM((N_PAGE_BUFS, n_pages_step * LAT_C, PAGE), F32),
                        pltpu.SemaphoreType.DMA((N_PAGE_BUFS,))],
    )
    return pl.pallas_call(
        functools.partial(_mla_sample_kernel, n_pages_step=n_pages_step, n_steps=n_steps, n_seq=db),
        grid_spec=grid_spec,
        out_shape=jax.ShapeDtypeStruct((db, 8, H_C * V_C), F32),
        compiler_params=_params("arbitrary", "arbitrary"),
        name="mla_sample",
    )(page_table, qbd, qrope, qrows, knew, cnew, wkt, wkg, wv, cache)


def _group_mean_matrix(sizes, total):
    m = jnp.zeros((total, total), F32)
    off = 0
    for sz, live in sizes:
        if live:
            m = m.at[off:off + sz, off:off + sz].set(1.0 / sz)
        off += sz
    return m.astype(BF16)


def _rope_tables(pos, lane_offset):
    half = ROPE_C // 2
    inv = ROPE_THETA ** (-jnp.arange(half, dtype=F32) / half)
    ang = pos.astype(F32)[:, None] * inv[None, :]
    c2 = jnp.concatenate([jnp.cos(ang), jnp.cos(ang)], axis=1)
    s2 = jnp.concatenate([jnp.sin(ang), jnp.sin(ang)], axis=1)
    cos = jnp.ones((pos.shape[0], 128), F32).at[:, lane_offset:lane_offset + ROPE_C].set(c2)
    sin = jnp.zeros((pos.shape[0], 128), F32).at[:, lane_offset:lane_offset + ROPE_C].set(s2)
    return cos, sin


def kernel(x_prompt, x_sample, cache_kv_diff, cache_mla, state_lru_h, state_lru_conv, state_ffn_conv, page_table,
           norm_mix, norm_ffn, w_in_e, qk_norm_a, lambda_a, subln_a, conv_w_lru, conv_b_lru, gate_w_lru,
           gate_b_lru, lambda_lru, w_out_e, w_in_c, q_a_norm, kv_a_norm, w_uq, w_ukv, qn_c, rn_c, w_out_c,
           w_up, conv_w_ffn, conv_b_ffn, w_down):
    B, T, D = x_prompt.shape
    DB, S, _ = x_sample.shape
    n_pages = page_table.shape[1]
    n_past = n_pages * PAGE
    n_pool = cache_kv_diff.shape[1]
    depth = norm_mix.shape[0]
    NP, NS = B * T, S * DB
    pt = page_table.astype(jnp.int32)

    xp = x_prompt.reshape(NP, D)
    xs = jnp.swapaxes(x_sample, 0, 1).reshape(NS, D)

    def to_seq_major(a):
        return jnp.swapaxes(a.reshape((S, DB) + a.shape[1:]), 0, 1)

    pos_p = jnp.arange(T, dtype=jnp.int32)
    pos_s = jnp.repeat(n_past + jnp.arange(S, dtype=jnp.int32), DB)
    tabs_p = _rope_tables(pos_p, NOPE_C) + _rope_tables(pos_p, 0)
    tabs_s = _rope_tables(pos_s, NOPE_C) + _rope_tables(pos_s, 0)

    kv_p, kv_s, mla_p, mla_s = [], [], [], []
    lh_p, lh_s, lc_p, lc_s, fc_p, fc_s = [], [], [], [], [], []

    for l in range(depth):
        g_mix = norm_mix[l].reshape(1, D)
        if l % 2 == 0:
            e = l // 2
            lam_init = 0.8 - 0.6 * math.exp(-0.3 * l)
            w_in = w_in_e[e].astype(BF16)
            qs = jnp.tile(qk_norm_a[e, 0].reshape(1, 2 * HD_A), (1, H_A)) * (1.0 / math.sqrt(HD_A))
            ks = jnp.tile(qk_norm_a[e, 1].reshape(1, 2 * HD_A), (1, H_A))
            gm64 = _group_mean_matrix([(HD_A, True)] * (2 * H_A), 2 * H_A * HD_A)
            lam_v = lambda_a[e]
            sg = subln_a[e].reshape(1, 2 * HD_A)
            cw, cb = conv_w_lru[e], conv_b_lru[e].reshape(1, D_RNN)
            gw = jnp.zeros((2, 8, 64, 8, 64), F32)
            gw = gw.at[:, jnp.arange(8), :, jnp.arange(8), :].set(jnp.swapaxes(gate_w_lru[e], 0, 1))
            gw = jnp.concatenate([gw[0].reshape(D_RNN, D_RNN), gw[1].reshape(D_RNN, D_RNN)], axis=1).astype(BF16)
            gb = gate_b_lru[e].reshape(1, 2 * D_RNN)
            lam_lru = lambda_lru[e].reshape(1, D_RNN)
            w_out = w_out_e[e].astype(BF16)

            qb_p, kb_p, vb_p, kvf_p, xg_p = _even_proj(xp, g_mix, w_in, qs, ks, gm64, 512)
            qb_s, kb_s, vb_s, kvf_s, xg_s = _even_proj(xs, g_mix, w_in, qs, ks, gm64, 512)

            a_p = _flash(qb_p, kb_p, vb_p, (lam_v, sg), diff=True, batch=B, seq=T, tb=512, npair=2, lam_init=lam_init)

            q_sm = to_seq_major(qb_s.astype(F32)).reshape(DB, S, H_A, 2, HD_A)
            eye_h = jnp.eye(H_A, dtype=F32)
            eye_m = jnp.eye(2, dtype=F32)
            qbd = jnp.einsum('bqhmd,hg,mn->bmhqgnd', q_sm, eye_h, eye_m)
            qbd = qbd.reshape(DB, 2 * H_A * S, H_A * 2 * HD_A)
            kvf_s = kvf_s.reshape(NS, 2 * H_A * 2 * HD_A)
            knew = jnp.pad(to_seq_major(kvf_s), ((0, 0), (0, 8 - S), (0, 0)))
            cache_d = cache_kv_diff.reshape(cache_kv_diff.shape[0] * n_pool * PAGE * 2 * H_A, 2 * HD_A)
            sg4 = jnp.tile(sg, (1, H_A))
            a_s = _diff_sample(pt + e * n_pool, qbd, knew, lam_v, sg4, cache_d,
                               n_pages_step=min(16, n_pages), lam_init=lam_init)
            a_s = jnp.swapaxes(a_s, 0, 1).reshape(NS, H_A * 2 * HD_A).astype(BF16)

            y_p, hl_pp, ct_pp = _lru_prompt(xg_p, cw, cb, gw, gb, lam_lru, batch=B, seq=T, tm=256)
            hist = jnp.swapaxes(state_lru_conv[e], 0, 1).reshape(3 * DB, D_RNN)
            y_s, hl_ss, ct_ss = _lru_sample(xg_s, hist, state_lru_h[e], cw, cb, gw, gb, lam_lru, db=DB, steps=S)

            xp = _mm_res(xp, [a_p, y_p], w_out, 512)
            xs = _mm_res(xs, [a_s, y_s], w_out, 512)

            kv_p.append(kvf_p.reshape(B, T, 2, H_A, 2 * HD_A))
            kv_s.append(to_seq_major(kvf_s).reshape(DB, S, 2, H_A, 2 * HD_A))
            nt = hl_pp.shape[0] // B
            lh_p.append(hl_pp.reshape(B, nt, 8, D_RNN)[:, -1, -1])
            lc_p.append(ct_pp.reshape(B, nt, 8, D_RNN)[:, -1, 5:8])
            lh_s.append(hl_ss)
            lc_s.append(jnp.swapaxes(ct_ss.reshape(3, DB, D_RNN), 0, 1))
        else:
            o = l // 2
            scale = 1.0 / math.sqrt(NOPE_C + ROPE_C)
            pad_h = HEAD_PAD - NOPE_C - ROPE_C
            win = jnp.pad(w_in_c[o], ((0, 0), (0, 512 - w_in_c.shape[2]))).astype(BF16)
            qag = q_a_norm[o].reshape(1, Q_RANK)
            kvg = kv_a_norm[o].reshape(1, KV_RANK)
            rkg = jnp.pad(rn_c[o, 1], (0, 128 - ROPE_C)).reshape(1, 128)
            wq = jnp.pad(w_uq[o], ((0, 0), (0, 0), (0, pad_h))).reshape(Q_RANK, H_C * HEAD_PAD).astype(BF16)
            gq = jnp.tile(jnp.concatenate([qn_c[o, 0], rn_c[o, 0], jnp.zeros((pad_h,), F32)]) * scale,
                          H_C).reshape(1, H_C * HEAD_PAD)
            gm256 = _group_mean_matrix([(NOPE_C, True), (ROPE_C, True), (pad_h, False)] * 2, 256)
            wk_nope = w_ukv[o][:, :, :NOPE_C]
            wk = jnp.pad(wk_nope, ((0, 0), (0, 0), (0, HEAD_PAD - NOPE_C))).reshape(KV_RANK, H_C * HEAD_PAD)
            wk = wk.astype(BF16)
            gk = jnp.tile(jnp.concatenate([qn_c[o, 1], jnp.zeros((HEAD_PAD - NOPE_C,), F32)]),
                          H_C).reshape(1, H_C * HEAD_PAD)
            wv = w_ukv[o][:, :, NOPE_C:].reshape(KV_RANK, H_C * V_C).astype(BF16)
            wkt = jnp.transpose(wk_nope.reshape(KV_RANK, H_C * NOPE_C))
            wkg = (wkt * jnp.tile(qn_c[o, 1], H_C)[:, None]).astype(BF16)
            wkt = wkt.astype(BF16)
            w_out = w_out_c[o].astype(BF16)

            tm_p = _row_tile(T, 512)
            qh_p, kh_p, v_p, rows_p = _mla_proj(xp, g_mix, win, qag, kvg, rkg, wq, gq, gm256, wk, gk, wv, tabs_p,
                                                tab_tiles=T // tm_p, tm=tm_p)
            qh_s, kh_s, v_s, rows_s = _mla_proj(xs, g_mix, win, qag, kvg, rkg, wq, gq, gm256, wk, gk, wv, tabs_s,
                                                tab_tiles=1, tm=NS)

            o_p = _flash(qh_p, kh_p, v_p, (), diff=False, batch=B, seq=T, tb=512, npair=2)

            qh_sm = to_seq_major(qh_s.astype(F32)).reshape(DB, S, H_C, HEAD_PAD)
            qrows = qh_sm.reshape(DB, S * H_C, HEAD_PAD)
            qn = qh_sm[..., :NOPE_C]
            qbd = jnp.einsum('bqhd,hg->bqhgd', qn, jnp.eye(H_C, dtype=F32)).reshape(DB, S * H_C, H_C * NOPE_C)
            qrope = jnp.pad(qh_sm[..., NOPE_C:NOPE_C + ROPE_C],
                            ((0, 0), (0, 0), (0, 0), (0, 128 - ROPE_C))).reshape(DB, S * H_C, 128)
            knew = to_seq_major(kh_s.astype(F32)).reshape(DB, S, H_C, HEAD_PAD)
            rows_sm = to_seq_major(rows_s)
            cnew = jnp.pad(rows_sm[..., :KV_RANK], ((0, 0), (0, 8 - S), (0, 0)))
            cache_m = jnp.swapaxes(cache_mla, 2, 3).reshape(cache_mla.shape[0] * n_pool * LAT_C, PAGE)
            o_s = _mla_sample(pt + o * n_pool, qbd.astype(BF16), qrope.astype(BF16), qrows, knew, cnew,
                              wkt, wkg, wv, cache_m, n_pages_step=min(16, n_pages))
            o_s = jnp.swapaxes(o_s[:, :S], 0, 1).reshape(NS, H_C * V_C).astype(BF16)

            xp = _mm_res(xp, [o_p], w_out, 512)
            xs = _mm_res(xs, [o_s], w_out, 512)
            mla_p.append(rows_p.reshape(B, T, LAT_C))
            mla_s.append(rows_sm)

        g_ffn = norm_ffn[l].reshape(1, D)
        wu = w_up[l].astype(BF16)
        wd = w_down[l].astype(BF16)
        cwf, cbf = conv_w_ffn[l], conv_b_ffn[l].reshape(1, 2 * D_FF)
        xp, tg, tl = _ffn_prompt(xp, g_ffn, wu, cwf, cbf, wd, seq=T, tm=1024, tf=1408)
        ntf = tg.shape[0] // B
        tail = jnp.concatenate([tg, tl], axis=-1).reshape(B, ntf, 8, 2 * D_FF)
        fc_p.append(tail[:, -1, 6:8])
        hist_f = jnp.swapaxes(state_ffn_conv[l], 0, 1).reshape(2 * DB, 2 * D_FF)
        xs, tgs, tls = _ffn_sample(xs, g_ffn, wu, cwf, cbf, wd, hist_f, db=DB, steps=S, tf=1408)
        fc_s.append(jnp.swapaxes(jnp.concatenate([tgs, tls], axis=-1).reshape(2, DB, 2 * D_FF), 0, 1))

    y_p = xp.reshape(B, T, D)
    y_s = to_seq_major(xs)
    return (y_p, y_s, jnp.stack(kv_p), jnp.stack(kv_s), jnp.stack(mla_p), jnp.stack(mla_s),
            jnp.stack(lh_p), jnp.stack(lh_s), jnp.stack(lc_p), jnp.stack(lc_s), jnp.stack(fc_p), jnp.stack(fc_s))
```
